```python
import math
import jax, jax.numpy as jnp
from jax import lax
import numpy as np

D_MODEL = 1024
BATCH = 4
SEQ = 4096
DEPTH = 2
DEC_BATCH = 32
DEC_SEQ = 4
PAST_LEN = 8192
PAGE_SIZE = 128

N_AB_LAYERS = (DEPTH + 1) // 2
N_C_LAYERS = DEPTH // 2
POOL_WIDTH = D_MODEL
POOL_WINDOWS = (2, 4, 8, 16)
POOL_GROUPS = len(POOL_WINDOWS)
POOL_GROUP_DIM = POOL_WIDTH // POOL_GROUPS
POOL_BUF = max(POOL_WINDOWS) - 1
CONV_WIDTH = D_MODEL
CONV_K = 31
CONV_BUF = CONV_K - 1
AB_IN = 2 * POOL_WIDTH + 3 * CONV_WIDTH
AB_MIX = POOL_WIDTH + CONV_WIDTH
C_HEADS = 8
C_HEAD_DIM = 64
C_V_DIM = 2 * C_HEAD_DIM
C_WIDTH = C_HEADS * C_V_DIM
C_IN = 4 * C_WIDTH
ATTN_SCALE = C_HEAD_DIM ** -0.5
Q_BLOCK = 128
EPS = 1e-6
NEG_INF = -1e30

kernel_name = "pool_conformer_diffattn_hybrid_step"


def rmsnorm(x, w):
    xf = x.astype(jnp.float32)
    y = xf * lax.rsqrt(jnp.mean(xf * xf, axis=-1, keepdims=True) + EPS)
    return (y * w.astype(jnp.float32)).astype(x.dtype)


def layernorm(x, w, b):
    xf = x.astype(jnp.float32)
    mu = jnp.mean(xf, axis=-1, keepdims=True)
    xc = xf - mu
    y = xc * lax.rsqrt(jnp.mean(xc * xc, axis=-1, keepdims=True) + EPS)
    return (y * w.astype(jnp.float32) + b.astype(jnp.float32)).astype(x.dtype)


def causal_multiscale_pool(full, n_valid):
    B, L, _ = full.shape
    T = L - POOL_BUF
    cs = jnp.cumsum(full.astype(jnp.float32), axis=1)
    cs = jnp.concatenate([jnp.zeros((B, 1, POOL_WIDTH), jnp.float32), cs], axis=1)
    hi = cs[:, POOL_BUF + 1:]
    pos = jnp.arange(T)
    means = []
    for g, w in enumerate(POOL_WINDOWS):
        sl = slice(g * POOL_GROUP_DIM, (g + 1) * POOL_GROUP_DIM)
        lo = cs[:, POOL_BUF + 1 - w: POOL_BUF + 1 - w + T, sl]
        cnt = jnp.minimum(w, pos + 1 + n_valid).astype(jnp.float32)
        means.append((hi[..., sl] - lo) / cnt[None, :, None])
    return jnp.concatenate(means, axis=-1)


def pool_conv_layer(x, pool_buf, conv_buf, n_valid, norm_w, w_in, pool_w, pool_scale,
                    conv_w, conv_b, ln_w, ln_b, w_out):
    B, T, _ = x.shape
    h = rmsnorm(x, norm_w)
    z = h @ w_in
    xa, ga, u, v, gb = jnp.split(
        z, [POOL_WIDTH, 2 * POOL_WIDTH, 2 * POOL_WIDTH + CONV_WIDTH, 2 * POOL_WIDTH + 2 * CONV_WIDTH], axis=-1)
    pool_full = jnp.concatenate([pool_buf, xa], axis=1)
    mean = causal_multiscale_pool(pool_full, n_valid)
    d = (mean - xa.astype(jnp.float32)).astype(x.dtype).reshape(B, T, POOL_GROUPS, POOL_GROUP_DIM)
    a = jnp.einsum('btgc,gce->btge', d, pool_w).reshape(B, T, POOL_WIDTH) * pool_scale
    glu = u * jax.nn.sigmoid(v)
    conv_full = jnp.concatenate([conv_buf, glu], axis=1)
    c = lax.conv_general_dilated(conv_full, conv_w[:, None, :], window_strides=(1,), padding='VALID',
                                 dimension_numbers=('NWC', 'WIO', 'NWC'),
                                 feature_group_count=CONV_WIDTH) + conv_b
    c = jax.nn.silu(layernorm(c, ln_w, ln_b))
    mix = jnp.concatenate([a * jax.nn.silu(ga), c * jax.nn.silu(gb)], axis=-1)
    y = x + mix @ w_out
    return y, pool_full[:, -POOL_BUF:], conv_full[:, -CONV_BUF:]


def diff_qkv(x, norm_w, w_in, qn_w, kn_w):
    B, T, _ = x.shape
    h = rmsnorm(x, norm_w)
    q, k, v, g = jnp.split(h @ w_in, 4, axis=-1)
    q = rmsnorm(q.reshape(B, T, C_HEADS, 2, C_HEAD_DIM), qn_w)
    k = rmsnorm(k.reshape(B, T, C_HEADS, 2, C_HEAD_DIM), kn_w)
    v = v.reshape(B, T, C_HEADS, C_V_DIM)
    return q, k, v, g


def diff_lambda(lq1, lk1, lq2, lk2, lambda_init):
    f32 = jnp.float32
    return (jnp.exp(jnp.sum(lq1.astype(f32) * lk1.astype(f32)))
            - jnp.exp(jnp.sum(lq2.astype(f32) * lk2.astype(f32))) + lambda_init)


def diff_attention(q, k, v, mask, lam):
    s = jnp.einsum('bqhcd,bkhcd->bhcqk', q, k, preferred_element_type=jnp.float32) * ATTN_SCALE
    s = jnp.where(mask[None, None, None], s, NEG_INF)
    p = jax.nn.softmax(s, axis=-1)
    p = p[:, :, 0] - lam * p[:, :, 1]
    return jnp.einsum('bhqk,bkhe->bqhe', p, v.astype(jnp.float32))


def prompt_diff_attention(q, k, v, lam):
    B, T = q.shape[:2]
    nb = T // Q_BLOCK
    qb = q.reshape(B, nb, Q_BLOCK, C_HEADS, 2, C_HEAD_DIM).swapaxes(0, 1)
    kpos = jnp.arange(T)

    def block(args):
        i, qi = args
        qpos = i * Q_BLOCK + jnp.arange(Q_BLOCK)
        return diff_attention(qi, k, v, kpos[None, :] <= qpos[:, None], lam)

    o = lax.map(block, (jnp.arange(nb), qb))
    return o.swapaxes(0, 1).reshape(B, T, C_HEADS, C_V_DIM)


def sample_diff_attention(q, k_new, v_new, cache_k, cache_v, page_table, lam):
    Bd, Tn = q.shape[:2]
    past = page_table.shape[1] * PAGE_SIZE
    k_past = cache_k[page_table].reshape(Bd, past, C_HEADS, 2, C_HEAD_DIM)
    v_past = cache_v[page_table].reshape(Bd, past, C_HEADS, C_V_DIM)
    k_all = jnp.concatenate([k_past, k_new], axis=1)
    v_all = jnp.concatenate([v_past, v_new], axis=1)
    kpos = jnp.arange(past + Tn)
    qpos = past + jnp.arange(Tn)
    return diff_attention(q, k_all, v_all, kpos[None, :] <= qpos[:, None], lam)


def diff_output(x, o, g, subln_w, lambda_init, w_out):
    B, T = x.shape[:2]
    o = rmsnorm(o, subln_w) * (1.0 - lambda_init)
    o = o.reshape(B, T, C_WIDTH).astype(x.dtype)
    return x + (o * jax.nn.silu(g)) @ w_out


def setup_inputs(seed: int = 0) -> dict:
    key = jax.random.key(seed)
    ks = jax.random.split(key, 32)
    f32 = jnp.float32
    n_pages = PAST_LEN // PAGE_SIZE
    n_used = DEC_BATCH * n_pages
    n_pool = (5 * n_used + 3) // 4

    def nrm(k, shape, s=1.0):
        return s * jax.random.normal(k, shape, f32)

    page_table = jax.random.permutation(ks[6], n_pool)[:n_used].reshape(DEC_BATCH, n_pages).astype(jnp.int32)
    return {
        "x_prompt": nrm(ks[0], (BATCH, SEQ, D_MODEL)),
        "x_sample": nrm(ks[1], (DEC_BATCH, DEC_SEQ, D_MODEL)),
        "state_pool": nrm(ks[2], (N_AB_LAYERS, DEC_BATCH, POOL_BUF, POOL_WIDTH)),
        "state_conv": nrm(ks[3], (N_AB_LAYERS, DEC_BATCH, CONV_BUF, CONV_WIDTH), 0.5),
        "cache_k": nrm(ks[4], (N_C_LAYERS, n_pool, PAGE_SIZE, C_HEADS, 2 * C_HEAD_DIM)),
        "cache_v": nrm(ks[5], (N_C_LAYERS, n_pool, PAGE_SIZE, C_HEADS, C_V_DIM)),
        "page_table": page_table,
        "norm_w_ab": 1.0 + nrm(ks[7], (N_AB_LAYERS, D_MODEL), 0.1),
        "w_in_ab": nrm(ks[8], (N_AB_LAYERS, D_MODEL, AB_IN), D_MODEL ** -0.5),
        "pool_w": nrm(ks[9], (N_AB_LAYERS, POOL_GROUPS, POOL_GROUP_DIM, POOL_GROUP_DIM), POOL_GROUP_DIM ** -0.5),
        "pool_scale": 1.0 + nrm(ks[10], (N_AB_LAYERS, POOL_WIDTH), 0.1),
        "conv_w": nrm(ks[11], (N_AB_LAYERS, CONV_K, CONV_WIDTH), CONV_K ** -0.5),
        "conv_b": nrm(ks[12], (N_AB_LAYERS, CONV_WIDTH), 0.01),
        "conv_ln_w": 1.0 + nrm(ks[13], (N_AB_LAYERS, CONV_WIDTH), 0.1),
        "conv_ln_b": nrm(ks[14], (N_AB_LAYERS, CONV_WIDTH), 0.01),
        "w_out_ab": nrm(ks[15], (N_AB_LAYERS, AB_MIX, D_MODEL), AB_MIX ** -0.5),
        "norm_w_c": 1.0 + nrm(ks[16], (N_C_LAYERS, D_MODEL), 0.1),
        "w_in_c": nrm(ks[17], (N_C_LAYERS, D_MODEL, C_IN), D_MODEL ** -0.5),
        "q_norm_w": 1.0 + nrm(ks[18], (N_C_LAYERS, C_HEAD_DIM), 0.1),
        "k_norm_w": 1.0 + nrm(ks[19], (N_C_LAYERS, C_HEAD_DIM), 0.1),
        "lambda_q1": nrm(ks[20], (N_C_LAYERS, C_HEAD_DIM), 0.1),
        "lambda_k1": nrm(ks[21], (N_C_LAYERS, C_HEAD_DIM), 0.1),
        "lambda_q2": nrm(ks[22], (N_C_LAYERS, C_HEAD_DIM), 0.1),
        "lambda_k2": nrm(ks[23], (N_C_LAYERS, C_HEAD_DIM), 0.1),
        "subln_w": 1.0 + nrm(ks[24], (N_C_LAYERS, C_V_DIM), 0.1),
        "w_out_c": nrm(ks[25], (N_C_LAYERS, C_WIDTH, D_MODEL), C_WIDTH ** -0.5),
    }


def reference(x_prompt, x_sample, state_pool, state_conv, cache_k, cache_v, page_table,
              norm_w_ab, w_in_ab, pool_w, pool_scale, conv_w, conv_b, conv_ln_w, conv_ln_b, w_out_ab,
              norm_w_c, w_in_c, q_norm_w, k_norm_w, lambda_q1, lambda_k1, lambda_q2, lambda_k2,
              subln_w, w_out_c):
    yp, ys = x_prompt, x_sample
    B, T = yp.shape[:2]
    Bd, Tn = ys.shape[:2]
    n_valid_sample = min(POOL_BUF, PAST_LEN)
    pool_p, conv_p, k_p, v_p = [], [], [], []
    pool_s, conv_s, k_s, v_s = [], [], [], []
    for l in range(DEPTH):
        j = l // 2
        if l % 2 == 0:
            params = (norm_w_ab[j], w_in_ab[j], pool_w[j], pool_scale[j], conv_w[j], conv_b[j],
                      conv_ln_w[j], conv_ln_b[j], w_out_ab[j])
            zero_pool = jnp.zeros((B, POOL_BUF, POOL_WIDTH), yp.dtype)
            zero_conv = jnp.zeros((B, CONV_BUF, CONV_WIDTH), yp.dtype)
            yp, sp, sc = pool_conv_layer(yp, zero_pool, zero_conv, 0, *params)
            ys, tp, tc = pool_conv_layer(ys, state_pool[j], state_conv[j], n_valid_sample, *params)
            pool_p.append(sp)
            conv_p.append(sc)
            pool_s.append(tp)
            conv_s.append(tc)
        else:
            lambda_init = 0.8 - 0.6 * math.exp(-0.3 * l)
            lam = diff_lambda(lambda_q1[j], lambda_k1[j], lambda_q2[j], lambda_k2[j], lambda_init)
            q, k, v, g = diff_qkv(yp, norm_w_c[j], w_in_c[j], q_norm_w[j], k_norm_w[j])
            o = prompt_diff_attention(q, k, v, lam)
            yp = diff_output(yp, o, g, subln_w[j], lambda_init, w_out_c[j])
            k_p.append(k.reshape(B, T, C_HEADS, 2 * C_HEAD_DIM))
            v_p.append(v)
            q, k, v, g = diff_qkv(ys, norm_w_c[j], w_in_c[j], q_norm_w[j], k_norm_w[j])
            o = sample_diff_attention(q, k, v, cache_k[j], cache_v[j], page_table, lam)
            ys = diff_output(ys, o, g, subln_w[j], lambda_init, w_out_c[j])
            k_s.append(k.reshape(Bd, Tn, C_HEADS, 2 * C_HEAD_DIM))
            v_s.append(v)
    return (yp, ys, jnp.stack(pool_p), jnp.stack(conv_p), jnp.stack(k_p), jnp.stack(v_p),
            jnp.stack(pool_s), jnp.stack(conv_s), jnp.stack(k_s), jnp.stack(v_s))
```

```python
import functools
import math

import jax
import jax.numpy as jnp
from jax import lax
from jax.experimental import pallas as pl
from jax.experimental.pallas import tpu as pltpu

F32 = jnp.float32
BF16 = jnp.bfloat16

POOL_WINDOWS = (2, 4, 8, 16)
POOL_BUF = max(POOL_WINDOWS) - 1
CONV_K = 31
CONV_BUF = CONV_K - 1
N_HEADS = 8
HEAD_DIM = 64
V_DIM = 2 * HEAD_DIM
PAGE = 128
EPS = 1e-6
NEG_INF = -1e30
ATTN_SCALE = HEAD_DIM ** -0.5

SUBLANES = 8
LANES = 128
VMEM_LIMIT = 56 * 1024 * 1024

POOL_HALO = 16
CONV_HALO = 32
CONV_ROWS = 64
CONV_LANES = 256


def _sigmoid(x):
    return 1.0 / (1.0 + jnp.exp(-x))


def _silu(x):
    return x * _sigmoid(x)


def _rms_rows(x, w):
    ms = jnp.mean(x * x, axis=-1, keepdims=True)
    return x * lax.rsqrt(ms + EPS) * w


def _dot(a, b):
    return jnp.dot(a, b, preferred_element_type=F32)


def _dot_nt(a, b):
    return lax.dot_general(a, b, (((1,), (1,)), ((), ())), preferred_element_type=F32)


def _const_spec(shape):
    n = len(shape)
    return pl.BlockSpec(shape, lambda *_: (0,) * n)


def _params(semantics):
    return pltpu.CompilerParams(dimension_semantics=semantics, vmem_limit_bytes=VMEM_LIMIT)


def _layernorm_silu(c, ln_w, ln_b):
    mu = jnp.mean(c, axis=-1, keepdims=True)
    xc = c - mu
    var = jnp.mean(xc * xc, axis=-1, keepdims=True)
    return _silu(xc * lax.rsqrt(var + EPS) * ln_w + ln_b)


def _ab_prompt_kernel(x_ref, nw_ref, win_ref, pw_ref, ps_ref, cw_ref, cb_ref, lnw_ref, lnb_ref, wout_ref,
                      y_ref, ptail_ref, ctail_ref, xbuf, gbuf, cbuf, *, tt, width):
    t = pl.program_id(1)
    nt = pl.num_programs(1)
    group = width // len(POOL_WINDOWS)

    @pl.when(t == 0)
    def _():
        xbuf[0:POOL_HALO, :] = jnp.zeros((POOL_HALO, width), F32)
        gbuf[0:CONV_HALO, :] = jnp.zeros((CONV_HALO, width), F32)

    x = x_ref[0]
    h = _rms_rows(x, nw_ref[...]).astype(BF16)

    xa = _dot(h, win_ref[:, 0:width])
    xbuf[POOL_HALO:POOL_HALO + tt, :] = xa
    pos = t * tt + lax.broadcasted_iota(jnp.int32, (tt, 1), 0)
    a_parts = []
    for g, w in enumerate(POOL_WINDOWS):
        sl = slice(g * group, (g + 1) * group)
        acc = xa[:, sl]
        for j in range(1, w):
            acc = acc + xbuf[POOL_HALO - j:POOL_HALO - j + tt, sl]
        cnt = jnp.minimum(w, pos + 1).astype(F32)
        d = (acc / cnt - xa[:, sl]).astype(BF16)
        a_parts.append(_dot(d, pw_ref[g]))
    a = jnp.concatenate(a_parts, axis=-1) * ps_ref[...]
    ga = _dot(h, win_ref[:, width:2 * width])
    mix_a = (a * _silu(ga)).astype(BF16)

    u = _dot(h, win_ref[:, 2 * width:3 * width])
    v = _dot(h, win_ref[:, 3 * width:4 * width])
    gbuf[CONV_HALO:CONV_HALO + tt, :] = u * _sigmoid(v)

    first_row = CONV_HALO - CONV_BUF

    def conv_chunk(i, carry):
        r0 = pl.multiple_of(i * CONV_ROWS, CONV_ROWS)
        for lc in range(width // CONV_LANES):
            ls = slice(lc * CONV_LANES, (lc + 1) * CONV_LANES)
            win = gbuf[pl.ds(r0, CONV_ROWS + CONV_HALO), ls]
            acc = jnp.zeros((CONV_ROWS, CONV_LANES), F32)
            for r in range(SUBLANES):
                taps = [k for k in range(CONV_K) if (first_row + k) % SUBLANES == r]
                if not taps:
                    continue
                hi = max((first_row + k) // SUBLANES for k in taps) * SUBLANES + CONV_ROWS
                shifted = win[r:r + hi, :]
                for k in taps:
                    o = (first_row + k) // SUBLANES * SUBLANES
                    acc = acc + shifted[o:o + CONV_ROWS, :] * cw_ref[k:k + 1, ls]
            cbuf[pl.ds(r0, CONV_ROWS), ls] = acc
        return carry

    lax.fori_loop(0, tt // CONV_ROWS, conv_chunk, 0)
    c = _layernorm_silu(cbuf[...] + cb_ref[...], lnw_ref[...], lnb_ref[...])
    gb = _dot(h, win_ref[:, 4 * width:5 * width])
    mix_b = (c * _silu(gb)).astype(BF16)

    y_ref[0] = x + _dot(mix_a, wout_ref[0:width, :]) + _dot(mix_b, wout_ref[width:2 * width, :])

    @pl.when(t == nt - 1)
    def _():
        ptail_ref[0] = xbuf[POOL_HALO + tt - POOL_BUF:POOL_HALO + tt, :]
        ctail_ref[0] = gbuf[CONV_HALO + tt - CONV_BUF:CONV_HALO + tt, :]

    xbuf[0:POOL_HALO, :] = xbuf[tt:tt + POOL_HALO, :]
    gbuf[0:CONV_HALO, :] = gbuf[tt:tt + CONV_HALO, :]


def _ab_prompt(x, nw, win, pw, ps, cw, cb, lnw, lnb, wout, *, tt):
    b, t, d = x.shape
    width = ps.shape[-1]
    kern = functools.partial(_ab_prompt_kernel, tt=tt, width=width)
    tile = pl.BlockSpec((1, tt, d), lambda i, j: (i, j, 0))
    return pl.pallas_call(
        kern,
        grid=(b, t // tt),
        in_specs=[tile, _const_spec(nw.shape), _const_spec(win.shape), _const_spec(pw.shape),
                  _const_spec(ps.shape), _const_spec(cw.shape), _const_spec(cb.shape),
                  _const_spec(lnw.shape), _const_spec(lnb.shape), _const_spec(wout.shape)],
        out_specs=[tile,
                   pl.BlockSpec((1, POOL_BUF, width), lambda i, j: (i, 0, 0)),
                   pl.BlockSpec((1, CONV_BUF, width), lambda i, j: (i, 0, 0))],
        out_shape=[jax.ShapeDtypeStruct((b, t, d), F32),
                   jax.ShapeDtypeStruct((b, POOL_BUF, width), F32),
                   jax.ShapeDtypeStruct((b, CONV_BUF, width), F32)],
        scratch_shapes=[pltpu.VMEM((POOL_HALO + tt, width), F32),
                        pltpu.VMEM((CONV_HALO + tt, width), F32),
                        pltpu.VMEM((tt, width), F32)],
        compiler_params=_params(("arbitrary", "arbitrary")),
        name="ab_prompt",
    )(x, nw, win, pw, ps, cw, cb, lnw, lnb, wout)


def _ab_sample_kernel(x_ref, sp_ref, sc_ref, nw_ref, win_ref, pw_ref, ps_ref, cw_ref, cb_ref, lnw_ref, lnb_ref,
                      wout_ref, y_ref, xa_ref, glu_ref, *, tn, nb, width):
    group = width // len(POOL_WINDOWS)
    x = x_ref[...].reshape(tn * nb, x_ref.shape[-1])
    h = _rms_rows(x, nw_ref[...]).astype(BF16)

    xa = _dot(h, win_ref[:, 0:width])
    xa_ref[...] = xa.reshape(tn, nb, width)

    def pool_row(j):
        return sp_ref[j] if j < POOL_BUF else xa[(j - POOL_BUF) * nb:(j - POOL_BUF + 1) * nb, :]

    d_rows = []
    for ti in range(tn):
        parts = []
        for g, w in enumerate(POOL_WINDOWS):
            sl = slice(g * group, (g + 1) * group)
            acc = pool_row(POOL_BUF + ti)[:, sl]
            for j in range(1, w):
                acc = acc + pool_row(POOL_BUF + ti - j)[:, sl]
            cnt = float(min(w, ti + 1 + POOL_BUF))
            parts.append(acc / cnt - pool_row(POOL_BUF + ti)[:, sl])
        d_rows.append(jnp.concatenate(parts, axis=-1))
    d = jnp.concatenate(d_rows, axis=0).astype(BF16)
    a = jnp.concatenate([_dot(d[:, g * group:(g + 1) * group], pw_ref[g]) for g in range(len(POOL_WINDOWS))],
                        axis=-1) * ps_ref[...]
    ga = _dot(h, win_ref[:, width:2 * width])
    mix_a = (a * _silu(ga)).astype(BF16)

    u = _dot(h, win_ref[:, 2 * width:3 * width])
    v = _dot(h, win_ref[:, 3 * width:4 * width])
    glu = u * _sigmoid(v)
    glu_ref[...] = glu.reshape(tn, nb, width)

    def conv_row(j):
        return sc_ref[j] if j < CONV_BUF else glu[(j - CONV_BUF) * nb:(j - CONV_BUF + 1) * nb, :]

    c_rows = []
    for ti in range(tn):
        acc = jnp.zeros((nb, width), F32)
        for k in range(CONV_K):
            acc = acc + conv_row(ti + k) * cw_ref[k:k + 1, :]
        c_rows.append(acc)
    c = _layernorm_silu(jnp.concatenate(c_rows, axis=0) + cb_ref[...], lnw_ref[...], lnb_ref[...])
    gb = _dot(h, win_ref[:, 4 * width:5 * width])
    mix_b = (c * _silu(gb)).astype(BF16)

    y = x + _dot(mix_a, wout_ref[0:width, :]) + _dot(mix_b, wout_ref[width:2 * width, :])
    y_ref[...] = y.reshape(tn, nb, y_ref.shape[-1])


def _ab_sample(x_t, sp_t, sc_t, nw, win, pw, ps, cw, cb, lnw, lnb, wout):
    tn, nb, d = x_t.shape
    width = ps.shape[-1]
    kern = functools.partial(_ab_sample_kernel, tn=tn, nb=nb, width=width)
    args = (x_t, sp_t, sc_t, nw, win, pw, ps, cw, cb, lnw, lnb, wout)
    return pl.pallas_call(
        kern,
        grid=(1,),
        in_specs=[_const_spec(a.shape) for a in args],
        out_specs=[_const_spec((tn, nb, d)), _const_spec((tn, nb, width)), _const_spec((tn, nb, width))],
        out_shape=[jax.ShapeDtypeStruct((tn, nb, d), F32),
                   jax.ShapeDtypeStruct((tn, nb, width), F32),
                   jax.ShapeDtypeStruct((tn, nb, width), F32)],
        compiler_params=_params(("arbitrary",)),
        name="ab_sample",
    )(*args)


def _subhead_norm(x, w2):
    lane = lax.broadcasted_iota(jnp.int32, x.shape, 1)
    lo = lane < HEAD_DIM
    sq = x * x
    s_lo = jnp.sum(jnp.where(lo, sq, 0.0), axis=-1, keepdims=True)
    s_hi = jnp.sum(jnp.where(lo, 0.0, sq), axis=-1, keepdims=True)
    ms = jnp.where(lo, s_lo, s_hi) * (1.0 / HEAD_DIM)
    return x * lax.rsqrt(ms + EPS) * w2


def _qkv_kernel(x_ref, nw_ref, win_ref, qw_ref, kw_ref, *out_refs, cw, head_major):
    x = x_ref[...]
    h = _rms_rows(x, nw_ref[...]).astype(BF16)
    q = _dot(h, win_ref[:, 0:cw])
    k = _dot(h, win_ref[:, cw:2 * cw])
    v = _dot(h, win_ref[:, 2 * cw:3 * cw])
    g = _dot(h, win_ref[:, 3 * cw:4 * cw])
    if head_major:
        k_ref, v_ref, g_ref, q1_ref, q2_ref, kb_ref, vb_ref = out_refs
    else:
        k_ref, v_ref, g_ref, q_ref = out_refs
    v_ref[...] = v
    g_ref[...] = g
    for hd in range(N_HEADS):
        sl = slice(hd * V_DIM, (hd + 1) * V_DIM)
        qn = _subhead_norm(q[:, sl], qw_ref[...]) * ATTN_SCALE
        kn = _subhead_norm(k[:, sl], kw_ref[...])
        k_ref[:, sl] = kn
        if head_major:
            lo = lax.broadcasted_iota(jnp.int32, qn.shape, 1) < HEAD_DIM
            q1_ref[hd] = jnp.where(lo, qn, 0.0).astype(BF16)
            q2_ref[hd] = jnp.where(lo, 0.0, qn).astype(BF16)
            kb_ref[hd] = kn.astype(BF16)
            vb_ref[hd] = v[:, sl].astype(BF16)
        else:
            q_ref[:, sl] = qn


def _qkv(x2d, nw, win, qw2, kw2, *, tt, head_major):
    n, d = x2d.shape
    cw = win.shape[1] // 4
    tile = pl.BlockSpec((tt, d), lambda i: (i, 0))
    wide = pl.BlockSpec((tt, cw), lambda i: (i, 0))
    hm = pl.BlockSpec((N_HEADS, tt, V_DIM), lambda i: (0, i, 0))
    out_specs = [wide, wide, wide]
    out_shape = [jax.ShapeDtypeStruct((n, cw), F32)] * 3
    if head_major:
        out_specs += [hm] * 4
        out_shape += [jax.ShapeDtypeStruct((N_HEADS, n, V_DIM), BF16)] * 4
    else:
        out_specs += [wide]
        out_shape += [jax.ShapeDtypeStruct((n, cw), F32)]
    return pl.pallas_call(
        functools.partial(_qkv_kernel, cw=cw, head_major=head_major),
        grid=(n // tt,),
        in_specs=[tile, _const_spec(nw.shape), _const_spec(win.shape), _const_spec(qw2.shape),
                  _const_spec(kw2.shape)],
        out_specs=out_specs,
        out_shape=out_shape,
        compiler_params=_params(("arbitrary",)),
        name="qkv_prompt" if head_major else "qkv_sample",
    )(x2d, nw, win, qw2, kw2)


def _lambda(lq1_ref, lk1_ref, lq2_ref, lk2_ref, lambda_init):
    a = jnp.sum(lq1_ref[...] * lk1_ref[...], axis=-1, keepdims=True)
    b = jnp.sum(lq2_ref[...] * lk2_ref[...], axis=-1, keepdims=True)
    return jnp.exp(a) - jnp.exp(b) + lambda_init


def _attn_prompt_kernel(q1_ref, q2_ref, k_ref, v_ref, lq1_ref, lk1_ref, lq2_ref, lk2_ref, sw_ref, o_ref,
                        m1, l1, a1, m2, l2, a2, *, tq, lambda_init):
    qi = pl.program_id(2)
    q1 = q1_ref[0]
    q2 = q2_ref[0]

    for m, l, a in ((m1, l1, a1), (m2, l2, a2)):
        m[...] = jnp.full(m.shape, NEG_INF, F32)
        l[...] = jnp.zeros(l.shape, F32)
        a[...] = jnp.zeros(a.shape, F32)

    def block(j, masked):
        start = pl.multiple_of(j * tq, tq)
        k = k_ref[0, pl.ds(start, tq), :]
        v = v_ref[0, pl.ds(start, tq), :]
        for q, m, l, a in ((q1, m1, l1, a1), (q2, m2, l2, a2)):
            s = _dot_nt(q, k)
            if masked:
                row = lax.broadcasted_iota(jnp.int32, s.shape, 0)
                col = lax.broadcasted_iota(jnp.int32, s.shape, 1)
                s = jnp.where(col <= row, s, NEG_INF)
            m_old = m[...]
            m_new = jnp.maximum(m_old, jnp.max(s, axis=-1, keepdims=True))
            p = jnp.exp(s - m_new)
            alpha = jnp.exp(m_old - m_new)
            l[...] = alpha * l[...] + jnp.sum(p, axis=-1, keepdims=True)
            a[...] = alpha * a[...] + _dot(p.astype(BF16), v)
            m[...] = m_new

    def body(j, carry):
        block(j, False)
        return carry

    lax.fori_loop(0, qi, body, 0)
    block(qi, True)

    lam = _lambda(lq1_ref, lk1_ref, lq2_ref, lk2_ref, lambda_init)
    o = a1[...] / l1[...] - lam * (a2[...] / l2[...])
    o_ref[...] = _rms_rows(o, sw_ref[...]) * (1.0 - lambda_init)


def _attn_prompt(q1, q2, kb, vb, lq1, lk1, lq2, lk2, sw, *, nb, t, tq, lambda_init):
    nq = t // tq
    qspec = pl.BlockSpec((1, tq, V_DIM), lambda b, h, i: (h, b * nq + i, 0))
    kspec = pl.BlockSpec((1, t, V_DIM), lambda b, h, i: (h, b, 0))
    vec = _const_spec(lq1.shape)
    return pl.pallas_call(
        functools.partial(_attn_prompt_kernel, tq=tq, lambda_init=lambda_init),
        grid=(nb, N_HEADS, nq),
        in_specs=[qspec, qspec, kspec, kspec, vec, vec, vec, vec, _const_spec(sw.shape)],
        out_specs=pl.BlockSpec((tq, V_DIM), lambda b, h, i: (b * nq + i, h)),
        out_shape=jax.ShapeDtypeStruct((nb * t, N_HEADS * V_DIM), F32),
        scratch_shapes=[pltpu.VMEM((tq, 1), F32), pltpu.VMEM((tq, 1), F32), pltpu.VMEM((tq, V_DIM), F32)] * 2,
        compiler_params=_params(("arbitrary", "arbitrary", "arbitrary")),
        name="attn_prompt",
    )(q1, q2, kb, vb, lq1, lk1, lq2, lk2, sw)


def _attn_sample_kernel(pt_ref, q_ref, kn_ref, vn_ref, lq1_ref, lk1_ref, lq2_ref, lk2_ref, sw_ref, *refs,
                        pages, tn, lambda_init):
    k_refs = refs[:pages]
    v_refs = refs[pages:2 * pages]
    o_ref = refs[2 * pages]
    wq_f, wq, newk, newv, m_ref, l_ref, acc = refs[2 * pages + 1:]
    step = pl.program_id(1)
    rows = 2 * tn * N_HEADS
    cw = N_HEADS * V_DIM

    def update(s_list, v_list):
        m_old = m_ref[...]
        m_new = m_old
        for s in s_list:
            m_new = jnp.maximum(m_new, jnp.max(s, axis=-1, keepdims=True))
        alpha = jnp.exp(m_old - m_new)
        l_new = alpha * l_ref[...]
        a_new = alpha * acc[...]
        for s, v in zip(s_list, v_list):
            p = jnp.exp(s - m_new)
            l_new = l_new + jnp.sum(p, axis=-1, keepdims=True)
            a_new = a_new + _dot(p.astype(BF16), v)
        m_ref[...] = m_new
        l_ref[...] = l_new
        acc[...] = a_new

    @pl.when(step == 0)
    def _():
        sub = lax.broadcasted_iota(jnp.int32, (SUBLANES, cw), 0)
        lane = lax.broadcasted_iota(jnp.int32, (SUBLANES, cw), 1)
        same_head = (lane >> 7) == sub
        for c in range(2):
            sel = jnp.logical_and(same_head, ((lane >> 6) & 1) == c)
            for ti in range(tn):
                r0 = (c * tn + ti) * N_HEADS
                qrow = jnp.broadcast_to(q_ref[0, ti:ti + 1, :], (SUBLANES, cw))
                wq_f[r0:r0 + N_HEADS, :] = jnp.where(sel, qrow, 0.0)
        wq[...] = wq_f[...].astype(BF16)
        newk[...] = jnp.zeros(newk.shape, F32)
        newv[...] = jnp.zeros(newv.shape, F32)
        newk[0:tn, :] = kn_ref[0]
        newv[0:tn, :] = vn_ref[0]
        m_ref[...] = jnp.full(m_ref.shape, NEG_INF, F32)
        l_ref[...] = jnp.zeros(l_ref.shape, F32)
        acc[...] = jnp.zeros(acc.shape, F32)
        s = _dot_nt(wq[...], newk[...].astype(BF16))
        row = lax.broadcasted_iota(jnp.int32, s.shape, 0)
        col = lax.broadcasted_iota(jnp.int32, s.shape, 1)
        s = jnp.where(col <= (row >> 3) & (tn - 1), s, NEG_INF)
        update([s], [newv[...].astype(BF16)])

    w = wq[...]
    update([_dot_nt(w, kr[...].astype(BF16)) for kr in k_refs], [vr[...].astype(BF16) for vr in v_refs])

    @pl.when(step == pl.num_programs(1) - 1)
    def _():
        lam = _lambda(lq1_ref, lk1_ref, lq2_ref, lk2_ref, lambda_init)
        o = acc[...] / l_ref[...]
        half = rows // 2
        od = o[0:half, :] - lam * o[half:rows, :]
        sub = lax.broadcasted_iota(jnp.int32, od.shape, 0)
        lane = lax.broadcasted_iota(jnp.int32, od.shape, 1)
        od = jnp.where((lane >> 7) == (sub & (N_HEADS - 1)), od, 0.0)
        o4 = jnp.sum(od.reshape(tn, N_HEADS, cw), axis=1)
        for hd in range(N_HEADS):
            sl = slice(hd * V_DIM, (hd + 1) * V_DIM)
            o_ref[0, :, sl] = _rms_rows(o4[:, sl], sw_ref[...]) * (1.0 - lambda_init)


def _attn_sample(page_table, q, k_new, v_new, cache_k, cache_v, lq1, lk1, lq2, lk2, sw, *, pages, lambda_init):
    nb, tn, cw = q.shape
    n_pages = page_table.shape[1]
    rows = 2 * tn * N_HEADS
    tok = pl.BlockSpec((1, tn, cw), lambda b, s, pt: (b, 0, 0))

    def page_spec(i):
        return pl.BlockSpec((None, PAGE, cw), lambda b, s, pt: (pt[b * n_pages + s * pages + i], 0, 0))

    vec = pl.BlockSpec(lq1.shape, lambda b, s, pt: (0, 0))
    grid_spec = pltpu.PrefetchScalarGridSpec(
        num_scalar_prefetch=1,
        grid=(nb, n_pages // pages),
        in_specs=[tok, tok, tok, vec, vec, vec, vec, pl.BlockSpec(sw.shape, lambda b, s, pt: (0, 0))]
                 + [page_spec(i) for i in range(pages)] * 2,
        out_specs=tok,
        scratch_shapes=[pltpu.VMEM((rows, cw), F32), pltpu.VMEM((rows, cw), BF16),
                        pltpu.VMEM((PAGE, cw), F32), pltpu.VMEM((PAGE, cw), F32),
                        pltpu.VMEM((rows, 1), F32), pltpu.VMEM((rows, 1), F32), pltpu.VMEM((rows, cw), F32)],
    )
    return pl.pallas_call(
        functools.partial(_attn_sample_kernel, pages=pages, tn=tn, lambda_init=lambda_init),
        grid_spec=grid_spec,
        out_shape=jax.ShapeDtypeStruct((nb, tn, cw), F32),
        compiler_params=_params(("arbitrary", "arbitrary")),
        name="attn_sample",
    )(page_table.reshape(-1), q, k_new, v_new, lq1, lk1, lq2, lk2, sw,
      *([cache_k] * pages), *([cache_v] * pages))


def _out_kernel(x_ref, o_ref, g_ref, w_ref, y_ref):
    gated = (o_ref[...] * _silu(g_ref[...])).astype(BF16)
    y_ref[...] = x_ref[...] + _dot(gated, w_ref[...])


def _out_proj(x2d, o, g, w, *, tt):
    n, d = x2d.shape
    tile = pl.BlockSpec((tt, d), lambda i: (i, 0))
    wide = pl.BlockSpec((tt, w.shape[0]), lambda i: (i, 0))
    return pl.pallas_call(
        _out_kernel,
        grid=(n // tt,),
        in_specs=[tile, wide, wide, _const_spec(w.shape)],
        out_specs=tile,
        out_shape=jax.ShapeDtypeStruct((n, d), F32),
        compiler_params=_params(("arbitrary",)),
        name="out_proj",
    )(x2d, o, g, w)


def kernel(x_prompt, x_sample, state_pool, state_conv, cache_k, cache_v, page_table, norm_w_ab, w_in_ab, pool_w, pool_scale, conv_w, conv_b, conv_ln_w, conv_ln_b, w_out_ab, norm_w_c, w_in_c, q_norm_w, k_norm_w, lambda_q1, lambda_k1, lambda_q2, lambda_k2, subln_w, w_out_c):
    nb, t, d = x_prompt.shape
    nd, tn, _ = x_sample.shape
    depth = norm_w_ab.shape[0] + norm_w_c.shape[0]
    row = lambda a: a.reshape(1, -1)

    yp, ys = x_prompt, x_sample
    pool_p, conv_p, k_p, v_p = [], [], [], []
    pool_s, conv_s, k_s, v_s = [], [], [], []
    for l in range(depth):
        j = l // 2
        if l % 2 == 0:
            weights = (row(norm_w_ab[j]), w_in_ab[j].astype(BF16), pool_w[j].astype(BF16), row(pool_scale[j]),
                       conv_w[j], row(conv_b[j]), row(conv_ln_w[j]), row(conv_ln_b[j]), w_out_ab[j].astype(BF16))
            yp, ptail, ctail = _ab_prompt(yp, *weights, tt=256)
            pool_p.append(ptail)
            conv_p.append(ctail)
            ys_t, xa_t, glu_t = _ab_sample(ys.transpose(1, 0, 2), state_pool[j].transpose(1, 0, 2),
                                           state_conv[j].transpose(1, 0, 2), *weights)
            ys = ys_t.transpose(1, 0, 2)
            pool_s.append(jnp.concatenate([state_pool[j], xa_t.transpose(1, 0, 2)], axis=1)[:, -POOL_BUF:])
            conv_s.append(jnp.concatenate([state_conv[j], glu_t.transpose(1, 0, 2)], axis=1)[:, -CONV_BUF:])
        else:
            lambda_init = 0.8 - 0.6 * math.exp(-0.3 * l)
            nw = row(norm_w_c[j])
            win = w_in_c[j].astype(BF16)
            wout = w_out_c[j].astype(BF16)
            qw2 = row(jnp.tile(q_norm_w[j], 2))
            kw2 = row(jnp.tile(k_norm_w[j], 2))
            lams = (row(lambda_q1[j]), row(lambda_k1[j]), row(lambda_q2[j]), row(lambda_k2[j]))
            sw = row(subln_w[j])
            cw = win.shape[1] // 4
            yp2 = yp.reshape(nb * t, d)
            k, v, g, q1, q2, kb, vb = _qkv(yp2, nw, win, qw2, kw2, tt=256, head_major=True)
            o = _attn_prompt(q1, q2, kb, vb, *lams, sw, nb=nb, t=t, tq=512, lambda_init=lambda_init)
            yp = _out_proj(yp2, o, g, wout, tt=512).reshape(nb, t, d)
            k_p.append(k.reshape(nb, t, N_HEADS, V_DIM))
            v_p.append(v.reshape(nb, t, N_HEADS, V_DIM))
            ys2 = ys.reshape(nd * tn, d)
            k, v, g, q = _qkv(ys2, nw, win, qw2, kw2, tt=nd * tn, head_major=False)
            ck = cache_k[j].reshape(cache_k.shape[1], PAGE, cw)
            cv = cache_v[j].reshape(cache_v.shape[1], PAGE, cw)
            o = _attn_sample(page_table, q.reshape(nd, tn, cw), k.reshape(nd, tn, cw), v.reshape(nd, tn, cw),
                             ck, cv, *lams, sw, pages=8, lambda_init=lambda_init)
            ys = _out_proj(ys2, o.reshape(nd * tn, cw), g, wout, tt=nd * tn).reshape(nd, tn, d)
            k_s.append(k.reshape(nd, tn, N_HEADS, V_DIM))
            v_s.append(v.reshape(nd, tn, N_HEADS, V_DIM))
    return (yp, ys, jnp.stack(pool_p), jnp.stack(conv_p), jnp.stack(k_p), jnp.stack(v_p),
            jnp.stack(pool_s), jnp.stack(conv_s), jnp.stack(k_s), jnp.stack(v_s))
```

```python
import functools
import math

import jax
import jax.numpy as jnp
from jax import lax
from jax.experimental import pallas as pl
from jax.experimental.pallas import tpu as pltpu

F32 = jnp.float32
BF16 = jnp.bfloat16

POOL_WINDOWS = (2, 4, 8, 16)
POOL_BUF = max(POOL_WINDOWS) - 1
assert all(w & (w - 1) == 0 for w in POOL_WINDOWS)
CONV_K = 31
CONV_BUF = CONV_K - 1
N_HEADS = 8
HEAD_DIM = 64
V_DIM = 2 * HEAD_DIM
PAGE = 128
EPS = 1e-6
NEG_INF = -1e30
ATTN_SCALE = HEAD_DIM ** -0.5
LOG2E = math.log2(math.e)

SUBLANES = 8
LANES = 128
VMEM_LIMIT = 56 * 1024 * 1024

POOL_HALO = 16
CONV_HALO = 32
CONV_ROWS = 64
CONV_LANES = 256
KV_SPLIT = 2


def _sigmoid(x):
    return 1.0 / (1.0 + jnp.exp(-x))


def _silu(x):
    return x * _sigmoid(x)


def _rms_rows(x, w):
    ms = jnp.mean(x * x, axis=-1, keepdims=True)
    return x * lax.rsqrt(ms + EPS) * w


def _dot(a, b):
    return jnp.dot(a, b, preferred_element_type=F32)


def _dot_nt(a, b):
    return lax.dot_general(a, b, (((1,), (1,)), ((), ())), preferred_element_type=F32)


def _const_spec(shape):
    n = len(shape)
    return pl.BlockSpec(shape, lambda *_: (0,) * n)


def _params(semantics):
    return pltpu.CompilerParams(dimension_semantics=semantics, vmem_limit_bytes=VMEM_LIMIT)


def _layernorm_silu(c, ln_w, ln_b):
    mu = jnp.mean(c, axis=-1, keepdims=True)
    xc = c - mu
    var = jnp.mean(xc * xc, axis=-1, keepdims=True)
    return _silu(xc * lax.rsqrt(var + EPS) * ln_w + ln_b)


def _ab_prompt_kernel(x_ref, nw_ref, win_ref, pw_ref, ps_ref, cw_ref, cb_ref, lnw_ref, lnb_ref, wout_ref,
                      y_ref, ptail_ref, ctail_ref, xbuf, gbuf, cbuf, *, tt, width):
    t = pl.program_id(1)
    nt = pl.num_programs(1)
    group = width // len(POOL_WINDOWS)

    @pl.when(t == 0)
    def _():
        xbuf[0:POOL_HALO, :] = jnp.zeros((POOL_HALO, width), F32)
        gbuf[0:CONV_HALO, :] = jnp.zeros((CONV_HALO, width), F32)

    x = x_ref[0]
    h = _rms_rows(x, nw_ref[...]).astype(BF16)

    xa = _dot(h, win_ref[:, 0:width])
    xbuf[POOL_HALO:POOL_HALO + tt, :] = xa
    pos = t * tt + lax.broadcasted_iota(jnp.int32, (tt, 1), 0)
    a_parts = []
    for g, w in enumerate(POOL_WINDOWS):
        sl = slice(g * group, (g + 1) * group)
        acc = xbuf[:, sl]
        j = 1
        while j < w:
            acc = acc + pltpu.roll(acc, j, 0)
            j *= 2
        acc = acc[POOL_HALO:, :]
        cnt = jnp.minimum(w, pos + 1).astype(F32)
        d = (acc / cnt - xa[:, sl]).astype(BF16)
        a_parts.append(_dot(d, pw_ref[g]))
    a = jnp.concatenate(a_parts, axis=-1) * ps_ref[...]
    ga = _dot(h, win_ref[:, width:2 * width])
    mix_a = (a * _silu(ga)).astype(BF16)

    u = _dot(h, win_ref[:, 2 * width:3 * width])
    v = _dot(h, win_ref[:, 3 * width:4 * width])
    gbuf[CONV_HALO:CONV_HALO + tt, :] = u * _sigmoid(v)

    first_row = CONV_HALO - CONV_BUF

    def conv_chunk(i, carry):
        r0 = pl.multiple_of(i * CONV_ROWS, CONV_ROWS)
        for lc in range(width // CONV_LANES):
            ls = slice(lc * CONV_LANES, (lc + 1) * CONV_LANES)
            rows = CONV_ROWS + CONV_HALO
            win = gbuf[pl.ds(r0, rows), ls]
            acc = jnp.zeros((CONV_ROWS, CONV_LANES), F32)
            for r in range(SUBLANES):
                taps = [k for k in range(CONV_K) if (first_row + k) % SUBLANES == r]
                if not taps:
                    continue
                shifted = pltpu.roll(win, rows - r, 0) if r else win
                for k in taps:
                    o = (first_row + k) // SUBLANES * SUBLANES
                    acc = acc + shifted[o:o + CONV_ROWS, :] * cw_ref[k:k + 1, ls]
            cbuf[pl.ds(r0, CONV_ROWS), ls] = acc
        return carry

    lax.fori_loop(0, tt // CONV_ROWS, conv_chunk, 0)
    c = _layernorm_silu(cbuf[...] + cb_ref[...], lnw_ref[...], lnb_ref[...])
    gb = _dot(h, win_ref[:, 4 * width:5 * width])
    mix_b = (c * _silu(gb)).astype(BF16)

    y_ref[0] = x + _dot(mix_a, wout_ref[0:width, :]) + _dot(mix_b, wout_ref[width:2 * width, :])

    @pl.when(t == nt - 1)
    def _():
        ptail_ref[0] = xbuf[POOL_HALO + tt - POOL_BUF:POOL_HALO + tt, :]
        ctail_ref[0] = gbuf[CONV_HALO + tt - CONV_BUF:CONV_HALO + tt, :]

    xbuf[0:POOL_HALO, :] = xbuf[tt:tt + POOL_HALO, :]
    gbuf[0:CONV_HALO, :] = gbuf[tt:tt + CONV_HALO, :]


def _ab_prompt(x, nw, win, pw, ps, cw, cb, lnw, lnb, wout, *, tt):
    b, t, d = x.shape
    width = ps.shape[-1]
    kern = functools.partial(_ab_prompt_kernel, tt=tt, width=width)
    tile = pl.BlockSpec((1, tt, d), lambda i, j: (i, j, 0))
    return pl.pallas_call(
        kern,
        grid=(b, t // tt),
        in_specs=[tile, _const_spec(nw.shape), _const_spec(win.shape), _const_spec(pw.shape),
                  _const_spec(ps.shape), _const_spec(cw.shape), _const_spec(cb.shape),
                  _const_spec(lnw.shape), _const_spec(lnb.shape), _const_spec(wout.shape)],
        out_specs=[tile,
                   pl.BlockSpec((1, POOL_BUF, width), lambda i, j: (i, 0, 0)),
                   pl.BlockSpec((1, CONV_BUF, width), lambda i, j: (i, 0, 0))],
        out_shape=[jax.ShapeDtypeStruct((b, t, d), F32),
                   jax.ShapeDtypeStruct((b, POOL_BUF, width), F32),
                   jax.ShapeDtypeStruct((b, CONV_BUF, width), F32)],
        scratch_shapes=[pltpu.VMEM((POOL_HALO + tt, width), F32),
                        pltpu.VMEM((CONV_HALO + tt, width), F32),
                        pltpu.VMEM((tt, width), F32)],
        compiler_params=_params(("arbitrary", "arbitrary")),
        name="ab_prompt",
    )(x, nw, win, pw, ps, cw, cb, lnw, lnb, wout)


def _ab_sample_kernel(x_ref, sp_ref, sc_ref, nw_ref, win_ref, pw_ref, ps_ref, cw_ref, cb_ref, lnw_ref, lnb_ref,
                      wout_ref, y_ref, xa_ref, glu_ref, *, tn, nb, width):
    group = width // len(POOL_WINDOWS)
    x = x_ref[...].reshape(tn * nb, x_ref.shape[-1])
    h = _rms_rows(x, nw_ref[...]).astype(BF16)

    xa = _dot(h, win_ref[:, 0:width])
    xa_ref[...] = xa.reshape(tn, nb, width)

    def pool_row(j):
        return sp_ref[j] if j < POOL_BUF else xa[(j - POOL_BUF) * nb:(j - POOL_BUF + 1) * nb, :]

    d_rows = []
    for ti in range(tn):
        parts = []
        for g, w in enumerate(POOL_WINDOWS):
            sl = slice(g * group, (g + 1) * group)
            acc = pool_row(POOL_BUF + ti)[:, sl]
            for j in range(1, w):
                acc = acc + pool_row(POOL_BUF + ti - j)[:, sl]
            cnt = float(min(w, ti + 1 + POOL_BUF))
            parts.append(acc / cnt - pool_row(POOL_BUF + ti)[:, sl])
        d_rows.append(jnp.concatenate(parts, axis=-1))
    d = jnp.concatenate(d_rows, axis=0).astype(BF16)
    a = jnp.concatenate([_dot(d[:, g * group:(g + 1) * group], pw_ref[g]) for g in range(len(POOL_WINDOWS))],
                        axis=-1) * ps_ref[...]
    ga = _dot(h, win_ref[:, width:2 * width])
    mix_a = (a * _silu(ga)).astype(BF16)

    u = _dot(h, win_ref[:, 2 * width:3 * width])
    v = _dot(h, win_ref[:, 3 * width:4 * width])
    glu = u * _sigmoid(v)
    glu_ref[...] = glu.reshape(tn, nb, width)

    def conv_row(j):
        return sc_ref[j] if j < CONV_BUF else glu[(j - CONV_BUF) * nb:(j - CONV_BUF + 1) * nb, :]

    c_rows = []
    for ti in range(tn):
        acc = jnp.zeros((nb, width), F32)
        for k in range(CONV_K):
            acc = acc + conv_row(ti + k) * cw_ref[k:k + 1, :]
        c_rows.append(acc)
    c = _layernorm_silu(jnp.concatenate(c_rows, axis=0) + cb_ref[...], lnw_ref[...], lnb_ref[...])
    gb = _dot(h, win_ref[:, 4 * width:5 * width])
    mix_b = (c * _silu(gb)).astype(BF16)

    y = x + _dot(mix_a, wout_ref[0:width, :]) + _dot(mix_b, wout_ref[width:2 * width, :])
    y_ref[...] = y.reshape(tn, nb, y_ref.shape[-1])


def _ab_sample(x_t, sp_t, sc_t, nw, win, pw, ps, cw, cb, lnw, lnb, wout):
    tn, nb, d = x_t.shape
    width = ps.shape[-1]
    kern = functools.partial(_ab_sample_kernel, tn=tn, nb=nb, width=width)
    args = (x_t, sp_t, sc_t, nw, win, pw, ps, cw, cb, lnw, lnb, wout)
    return pl.pallas_call(
        kern,
        grid=(1,),
        in_specs=[_const_spec(a.shape) for a in args],
        out_specs=[_const_spec((tn, nb, d)), _const_spec((tn, nb, width)), _const_spec((tn, nb, width))],
        out_shape=[jax.ShapeDtypeStruct((tn, nb, d), F32),
                   jax.ShapeDtypeStruct((tn, nb, width), F32),
                   jax.ShapeDtypeStruct((tn, nb, width), F32)],
        compiler_params=_params(("arbitrary",)),
        name="ab_sample",
    )(*args)


def _subhead_norm(x, w2):
    lane = lax.broadcasted_iota(jnp.int32, x.shape, 1)
    lo = lane < HEAD_DIM
    sq = x * x
    s_lo = jnp.sum(jnp.where(lo, sq, 0.0), axis=-1, keepdims=True)
    s_hi = jnp.sum(jnp.where(lo, 0.0, sq), axis=-1, keepdims=True)
    ms = jnp.where(lo, s_lo, s_hi) * (1.0 / HEAD_DIM)
    return x * lax.rsqrt(ms + EPS) * w2


def _qkv_kernel(x_ref, nw_ref, win_ref, qw_ref, kw_ref, *out_refs, cw, head_major):
    x = x_ref[...]
    h = _rms_rows(x, nw_ref[...]).astype(BF16)
    q = _dot(h, win_ref[:, 0:cw])
    k = _dot(h, win_ref[:, cw:2 * cw])
    v = _dot(h, win_ref[:, 2 * cw:3 * cw])
    g = _dot(h, win_ref[:, 3 * cw:4 * cw])
    if head_major:
        k_ref, v_ref, g_ref, q1_ref, q2_ref, kb_ref, vb_ref = out_refs
    else:
        k_ref, v_ref, g_ref, q_ref = out_refs
    v_ref[...] = v
    g_ref[...] = g
    for hd in range(N_HEADS):
        sl = slice(hd * V_DIM, (hd + 1) * V_DIM)
        qn = _subhead_norm(q[:, sl], qw_ref[...]) * (ATTN_SCALE * LOG2E if head_major else ATTN_SCALE)
        kn = _subhead_norm(k[:, sl], kw_ref[...])
        k_ref[:, sl] = kn
        if head_major:
            qt = qn.T
            lo = lax.broadcasted_iota(jnp.int32, qt.shape, 0) < HEAD_DIM
            q1_ref[hd, 0] = jnp.where(lo, qt, 0.0).astype(BF16)
            q2_ref[hd, 0] = jnp.where(lo, 0.0, qt).astype(BF16)
            kb_ref[hd] = kn.astype(BF16)
            vb_ref[hd, 0] = v[:, sl].T.astype(BF16)
        else:
            q_ref[:, sl] = qn


def _qkv(x2d, nw, win, qw2, kw2, *, tt, head_major, attn_block=None):
    n, d = x2d.shape
    cw = win.shape[1] // 4
    tile = pl.BlockSpec((tt, d), lambda i: (i, 0))
    wide = pl.BlockSpec((tt, cw), lambda i: (i, 0))
    out_specs = [wide, wide, wide]
    out_shape = [jax.ShapeDtypeStruct((n, cw), F32)] * 3
    if head_major:
        per = attn_block // tt
        hm = pl.BlockSpec((N_HEADS, tt, V_DIM), lambda i: (0, i, 0))
        hmt = pl.BlockSpec((N_HEADS, 1, V_DIM, tt), lambda i: (0, i // per, 0, i % per))
        out_specs += [hmt, hmt, hm, hmt]
        tshape = jax.ShapeDtypeStruct((N_HEADS, n // attn_block, V_DIM, attn_block), BF16)
        out_shape += [tshape, tshape, jax.ShapeDtypeStruct((N_HEADS, n, V_DIM), BF16), tshape]
    else:
        out_specs += [wide]
        out_shape += [jax.ShapeDtypeStruct((n, cw), F32)]
    return pl.pallas_call(
        functools.partial(_qkv_kernel, cw=cw, head_major=head_major),
        grid=(n // tt,),
        in_specs=[tile, _const_spec(nw.shape), _const_spec(win.shape), _const_spec(qw2.shape),
                  _const_spec(kw2.shape)],
        out_specs=out_specs,
        out_shape=out_shape,
        compiler_params=_params(("arbitrary",)),
        name="qkv_prompt" if head_major else "qkv_sample",
    )(x2d, nw, win, qw2, kw2)


def _lambda(lq1_ref, lk1_ref, lq2_ref, lk2_ref, lambda_init):
    a = jnp.sum(lq1_ref[...] * lk1_ref[...], axis=-1, keepdims=True)
    b = jnp.sum(lq2_ref[...] * lk2_ref[...], axis=-1, keepdims=True)
    return jnp.exp(a) - jnp.exp(b) + lambda_init


def _fold_sublanes(x, op):
    shift = SUBLANES // 2
    while shift:
        x = op(x, pltpu.roll(x, shift, 0))
        shift //= 2
    return x


def _col_reduce(x3, op, reduce_leading):
    return _fold_sublanes(reduce_leading(x3, axis=0), op)


def _attn_prompt_kernel(q1_ref, q2_ref, k_ref, v_ref, lq1_ref, lk1_ref, lq2_ref, lk2_ref, sw_ref, o_ref,
                        m1, l1, a1, m2, l2, a2, *, tq, lambda_init):
    qi = pl.program_id(2)
    q1 = q1_ref[0, 0]
    q2 = q2_ref[0, 0]
    tkp = tq // KV_SPLIT
    nd = V_DIM // SUBLANES

    for m, l, a in ((m1, l1, a1), (m2, l2, a2)):
        m[...] = jnp.full(m.shape, NEG_INF, F32)
        l[...] = jnp.zeros(l.shape, F32)
        a[...] = jnp.zeros(a.shape, F32)

    def block(j, masked):
        chains = []
        ones = jnp.ones((2 * SUBLANES, tkp), BF16)
        for part in range(KV_SPLIT):
            lo = part * tkp
            k = k_ref[0, pl.ds(pl.multiple_of(j * tq, tq) + lo, tkp), :]
            vt = jnp.concatenate([v_ref[0, j, :, lo:lo + tkp], ones], axis=0)
            for q, state in ((q1, (m1, l1, a1)), (q2, (m2, l2, a2))):
                chains.append((_dot(k, q), vt, lo, state))
        for s, vt, lo, (m, l, a) in chains:
            if masked:
                key = lo + lax.broadcasted_iota(jnp.int32, s.shape, 0)
                qry = lax.broadcasted_iota(jnp.int32, s.shape, 1)
                s = jnp.where(key <= qry, s, NEG_INF)
            s3 = s.reshape(tkp // SUBLANES, SUBLANES, tq)
            m_old = m[...]
            m_new = jnp.maximum(m_old, _col_reduce(s3, jnp.maximum, jnp.max))
            p = jnp.exp2(s3 - m_new[None]).reshape(tkp, tq).astype(BF16)
            alpha = jnp.exp2(m_old - m_new)
            pv = _dot(vt, p)
            l[...] = alpha * l[...] + pv[V_DIM:V_DIM + SUBLANES, :]
            a[...] = (alpha[None] * a[...].reshape(nd, SUBLANES, tq)).reshape(V_DIM, tq) + pv[0:V_DIM, :]
            m[...] = m_new

    def body(j, carry):
        block(j, False)
        return carry

    lax.fori_loop(0, qi, body, 0)
    block(qi, True)

    lam = _lambda(lq1_ref, lk1_ref, lq2_ref, lk2_ref, lambda_init)
    o3 = (a1[...].reshape(nd, SUBLANES, tq) * (1.0 / l1[...])[None]
          - lam * (a2[...].reshape(nd, SUBLANES, tq) * (1.0 / l2[...])[None]))
    ms = _col_reduce(o3 * o3, jnp.add, jnp.sum) * (1.0 / V_DIM)
    on = (o3 * lax.rsqrt(ms + EPS)[None]).reshape(V_DIM, tq)
    o_ref[...] = on.T * sw_ref[...] * (1.0 - lambda_init)


def _attn_prompt(q1, q2, kb, vb, lq1, lk1, lq2, lk2, sw, *, nb, t, tq, lambda_init):
    nq = t // tq
    qspec = pl.BlockSpec((1, 1, V_DIM, tq), lambda b, h, i: (h, b * nq + i, 0, 0))
    kspec = pl.BlockSpec((1, t, V_DIM), lambda b, h, i: (h, b, 0))
    vspec = pl.BlockSpec((1, nq, V_DIM, tq), lambda b, h, i: (h, b, 0, 0))
    vec = _const_spec(lq1.shape)
    return pl.pallas_call(
        functools.partial(_attn_prompt_kernel, tq=tq, lambda_init=lambda_init),
        grid=(nb, N_HEADS, nq),
        in_specs=[qspec, qspec, kspec, vspec, vec, vec, vec, vec, _const_spec(sw.shape)],
        out_specs=pl.BlockSpec((tq, V_DIM), lambda b, h, i: (b * nq + i, h)),
        out_shape=jax.ShapeDtypeStruct((nb * t, N_HEADS * V_DIM), F32),
        scratch_shapes=[pltpu.VMEM((SUBLANES, tq), F32), pltpu.VMEM((SUBLANES, tq), F32),
                        pltpu.VMEM((V_DIM, tq), F32)] * 2,
        compiler_params=_params(("arbitrary", "arbitrary", "arbitrary")),
        name="attn_prompt",
    )(q1, q2, kb, vb, lq1, lk1, lq2, lk2, sw)


def _attn_sample_kernel(pt_ref, q_ref, kn_ref, vn_ref, lq1_ref, lk1_ref, lq2_ref, lk2_ref, sw_ref, *refs,
                        pages, tn, lambda_init):
    k_refs = refs[:pages]
    v_refs = refs[pages:2 * pages]
    o_ref = refs[2 * pages]
    wq, newk, newv, m_ref, l_ref, acc = refs[2 * pages + 1:]
    step = pl.program_id(1)
    half = tn * N_HEADS
    page_rows = PAGE * N_HEADS

    def same_head(shape):
        row = lax.broadcasted_iota(jnp.int32, shape, 0)
        col = lax.broadcasted_iota(jnp.int32, shape, 1)
        return (col & (N_HEADS - 1)) == (row & (N_HEADS - 1))

    def update(s_list, v_list):
        m_old = m_ref[...]
        m_new = m_old
        for s in s_list:
            m_new = jnp.maximum(m_new, jnp.max(s, axis=-1, keepdims=True))
        alpha = jnp.exp(m_old - m_new)
        l_new = alpha * l_ref[...]
        a_new = alpha * acc[...]
        for s, v in zip(s_list, v_list):
            p = jnp.exp(s - m_new)
            l_new = l_new + jnp.sum(p, axis=-1, keepdims=True)
            a_new = a_new + _dot(p.astype(BF16), v)
        m_ref[...] = m_new
        l_ref[...] = l_new
        acc[...] = a_new

    @pl.when(step == 0)
    def _():
        qb = q_ref[0]
        lo = lax.broadcasted_iota(jnp.int32, qb.shape, 1) < HEAD_DIM
        wq[...] = jnp.concatenate([jnp.where(lo, qb, 0.0), jnp.where(lo, 0.0, qb)], axis=0).astype(BF16)
        newk[...] = jnp.zeros(newk.shape, F32)
        newv[...] = jnp.zeros(newv.shape, F32)
        newk[0:half, :] = kn_ref[0]
        newv[0:half, :] = vn_ref[0]
        m_ref[...] = jnp.full(m_ref.shape, NEG_INF, F32)
        l_ref[...] = jnp.zeros(l_ref.shape, F32)
        acc[...] = jnp.zeros(acc.shape, F32)
        s = _dot_nt(wq[...], newk[...].astype(BF16))
        row = lax.broadcasted_iota(jnp.int32, s.shape, 0)
        col = lax.broadcasted_iota(jnp.int32, s.shape, 1)
        causal = (col >> 3) <= ((row >> 3) & (tn - 1))
        s = jnp.where(same_head(s.shape), jnp.where(causal, s, NEG_INF), NEG_INF)
        update([s], [newv[...].astype(BF16)])

    w = wq[...]
    mask = same_head((2 * half, page_rows))
    s_list = [jnp.where(mask, _dot_nt(w, kr[...].reshape(page_rows, V_DIM).astype(BF16)), NEG_INF)
              for kr in k_refs]
    update(s_list, [vr[...].reshape(page_rows, V_DIM).astype(BF16) for vr in v_refs])

    @pl.when(step == pl.num_programs(1) - 1)
    def _():
        lam = _lambda(lq1_ref, lk1_ref, lq2_ref, lk2_ref, lambda_init)
        o = acc[...] / l_ref[...]
        od = o[0:half, :] - lam * o[half:2 * half, :]
        o_ref[0] = _rms_rows(od, sw_ref[...]) * (1.0 - lambda_init)


def _attn_sample(page_table, q, k_new, v_new, cache_k, cache_v, lq1, lk1, lq2, lk2, sw, *, layer, pages,
                 lambda_init):
    nb, half, _ = q.shape
    tn = half // N_HEADS
    n_pages = page_table.shape[1]
    tok = pl.BlockSpec((1, half, V_DIM), lambda b, s, pt: (b, 0, 0))

    def page_spec(i):
        return pl.BlockSpec((None, None, PAGE, N_HEADS, V_DIM),
                            lambda b, s, pt: (layer, pt[b * n_pages + s * pages + i], 0, 0, 0))

    vec = pl.BlockSpec(lq1.shape, lambda b, s, pt: (0, 0))
    grid_spec = pltpu.PrefetchScalarGridSpec(
        num_scalar_prefetch=1,
        grid=(nb, n_pages // pages),
        in_specs=[tok, tok, tok, vec, vec, vec, vec, pl.BlockSpec(sw.shape, lambda b, s, pt: (0, 0))]
                 + [page_spec(i) for i in range(pages)] * 2,
        out_specs=tok,
        scratch_shapes=[pltpu.VMEM((2 * half, V_DIM), BF16),
                        pltpu.VMEM((LANES, V_DIM), F32), pltpu.VMEM((LANES, V_DIM), F32),
                        pltpu.VMEM((2 * half, 1), F32), pltpu.VMEM((2 * half, 1), F32),
                        pltpu.VMEM((2 * half, V_DIM), F32)],
    )
    return pl.pallas_call(
        functools.partial(_attn_sample_kernel, pages=pages, tn=tn, lambda_init=lambda_init),
        grid_spec=grid_spec,
        out_shape=jax.ShapeDtypeStruct((nb, half, V_DIM), F32),
        compiler_params=_params(("arbitrary", "arbitrary")),
        name="attn_sample",
    )(page_table.reshape(-1), q, k_new, v_new, lq1, lk1, lq2, lk2, sw,
      *([cache_k] * pages), *([cache_v] * pages))


def _out_kernel(x_ref, o_ref, g_ref, w_ref, y_ref):
    gated = (o_ref[...] * _silu(g_ref[...])).astype(BF16)
    y_ref[...] = x_ref[...] + _dot(gated, w_ref[...])


def _out_proj(x2d, o, g, w, *, tt):
    n, d = x2d.shape
    tile = pl.BlockSpec((tt, d), lambda i: (i, 0))
    wide = pl.BlockSpec((tt, w.shape[0]), lambda i: (i, 0))
    return pl.pallas_call(
        _out_kernel,
        grid=(n // tt,),
        in_specs=[tile, wide, wide, _const_spec(w.shape)],
        out_specs=tile,
        out_shape=jax.ShapeDtypeStruct((n, d), F32),
        compiler_params=_params(("arbitrary",)),
        name="out_proj",
    )(x2d, o, g, w)


def kernel(x_prompt, x_sample, state_pool, state_conv, cache_k, cache_v, page_table, norm_w_ab, w_in_ab, pool_w, pool_scale, conv_w, conv_b, conv_ln_w, conv_ln_b, w_out_ab, norm_w_c, w_in_c, q_norm_w, k_norm_w, lambda_q1, lambda_k1, lambda_q2, lambda_k2, subln_w, w_out_c):
    nb, t, d = x_prompt.shape
    nd, tn, _ = x_sample.shape
    depth = norm_w_ab.shape[0] + norm_w_c.shape[0]
    row = lambda a: a.reshape(1, -1)

    yp, ys = x_prompt, x_sample
    pool_p, conv_p, k_p, v_p = [], [], [], []
    pool_s, conv_s, k_s, v_s = [], [], [], []
    for l in range(depth):
        j = l // 2
        if l % 2 == 0:
            weights = (row(norm_w_ab[j]), w_in_ab[j].astype(BF16), pool_w[j].astype(BF16), row(pool_scale[j]),
                       conv_w[j], row(conv_b[j]), row(conv_ln_w[j]), row(conv_ln_b[j]), w_out_ab[j].astype(BF16))
            yp, ptail, ctail = _ab_prompt(yp, *weights, tt=256)
            pool_p.append(ptail)
            conv_p.append(ctail)
            ys_t, xa_t, glu_t = _ab_sample(ys.transpose(1, 0, 2), state_pool[j].transpose(1, 0, 2),
                                           state_conv[j].transpose(1, 0, 2), *weights)
            ys = ys_t.transpose(1, 0, 2)
            pool_s.append(jnp.concatenate([state_pool[j], xa_t.transpose(1, 0, 2)], axis=1)[:, -POOL_BUF:])
            conv_s.append(jnp.concatenate([state_conv[j], glu_t.transpose(1, 0, 2)], axis=1)[:, -CONV_BUF:])
        else:
            lambda_init = 0.8 - 0.6 * math.exp(-0.3 * l)
            nw = row(norm_w_c[j])
            win = w_in_c[j].astype(BF16)
            wout = w_out_c[j].astype(BF16)
            qw2 = row(jnp.tile(q_norm_w[j], 2))
            kw2 = row(jnp.tile(k_norm_w[j], 2))
            lams = (row(lambda_q1[j]), row(lambda_k1[j]), row(lambda_q2[j]), row(lambda_k2[j]))
            sw = row(subln_w[j])
            cw = win.shape[1] // 4
            yp2 = yp.reshape(nb * t, d)
            tq = 512
            k, v, g, q1, q2, kb, vb = _qkv(yp2, nw, win, qw2, kw2, tt=256, head_major=True, attn_block=tq)
            o = _attn_prompt(q1, q2, kb, vb, *lams, sw, nb=nb, t=t, tq=tq, lambda_init=lambda_init)
            yp = _out_proj(yp2, o, g, wout, tt=512).reshape(nb, t, d)
            k_p.append(k.reshape(nb, t, N_HEADS, V_DIM))
            v_p.append(v.reshape(nb, t, N_HEADS, V_DIM))
            ys2 = ys.reshape(nd * tn, d)
            k, v, g, q = _qkv(ys2, nw, win, qw2, kw2, tt=nd * tn, head_major=False)
            rows_th = lambda a: a.reshape(nd, tn * N_HEADS, V_DIM)
            o = _attn_sample(page_table, rows_th(q), rows_th(k), rows_th(v), cache_k, cache_v, *lams, sw,
                             layer=j, pages=8, lambda_init=lambda_init)
            ys = _out_proj(ys2, o.reshape(nd * tn, cw), g, wout, tt=nd * tn).reshape(nd, tn, d)
            k_s.append(k.reshape(nd, tn, N_HEADS, V_DIM))
            v_s.append(v.reshape(nd, tn, N_HEADS, V_DIM))
    return (yp, ys, jnp.stack(pool_p), jnp.stack(conv_p), jnp.stack(k_p), jnp.stack(v_p),
            jnp.stack(pool_s), jnp.stack(conv_s), jnp.stack(k_s), jnp.stack(v_s))
```

```python
import functools
import math

import jax
import jax.numpy as jnp
from jax import lax
from jax.experimental import pallas as pl
from jax.experimental.pallas import tpu as pltpu

F32 = jnp.float32
BF16 = jnp.bfloat16

POOL_WINDOWS = (2, 4, 8, 16)
POOL_BUF = max(POOL_WINDOWS) - 1
assert all(w & (w - 1) == 0 for w in POOL_WINDOWS)
CONV_K = 31
CONV_BUF = CONV_K - 1
N_HEADS = 8
HEAD_DIM = 64
V_DIM = 2 * HEAD_DIM
PAGE = 128
EPS = 1e-6
NEG_INF = -1e30
ATTN_SCALE = HEAD_DIM ** -0.5
LOG2E = math.log2(math.e)

SUBLANES = 8
LANES = 128
VMEM_LIMIT = 56 * 1024 * 1024

POOL_HALO = 16
CONV_HALO = 32
CONV_ROWS = 64
CONV_LANES = 256
KV_SPLIT = 2


def _sigmoid(x):
    return 1.0 / (1.0 + jnp.exp(-x))


def _silu(x):
    return x * _sigmoid(x)


def _rms_rows(x, w):
    ms = jnp.mean(x * x, axis=-1, keepdims=True)
    return x * lax.rsqrt(ms + EPS) * w


def _dot(a, b):
    return jnp.dot(a, b, preferred_element_type=F32)


def _dot_nt(a, b):
    return lax.dot_general(a, b, (((1,), (1,)), ((), ())), preferred_element_type=F32)


def _const_spec(shape):
    n = len(shape)
    return pl.BlockSpec(shape, lambda *_: (0,) * n)


def _params(semantics):
    return pltpu.CompilerParams(dimension_semantics=semantics, vmem_limit_bytes=VMEM_LIMIT)


def _layernorm_silu(c, ln_w, ln_b):
    mu = jnp.mean(c, axis=-1, keepdims=True)
    xc = c - mu
    var = jnp.mean(xc * xc, axis=-1, keepdims=True)
    return _silu(xc * lax.rsqrt(var + EPS) * ln_w + ln_b)


def _ab_prompt_kernel(x_ref, nw_ref, win_ref, pw_ref, ps_ref, cw_ref, cb_ref, lnw_ref, lnb_ref, wout_ref,
                      y_ref, ptail_ref, ctail_ref, xbuf, gbuf, cbuf, *, tt, width):
    t = pl.program_id(1)
    nt = pl.num_programs(1)
    group = width // len(POOL_WINDOWS)

    @pl.when(t == 0)
    def _():
        xbuf[0:POOL_HALO, :] = jnp.zeros((POOL_HALO, width), F32)
        gbuf[0:CONV_HALO, :] = jnp.zeros((CONV_HALO, width), F32)

    x = x_ref[0]
    h = _rms_rows(x, nw_ref[...]).astype(BF16)

    xa = _dot(h, win_ref[:, 0:width])
    xbuf[POOL_HALO:POOL_HALO + tt, :] = xa
    pos = t * tt + lax.broadcasted_iota(jnp.int32, (tt, 1), 0)
    a_parts = []
    for g, w in enumerate(POOL_WINDOWS):
        sl = slice(g * group, (g + 1) * group)
        acc = xbuf[:, sl]
        j = 1
        while j < w:
            acc = acc + pltpu.roll(acc, j, 0)
            j *= 2
        acc = acc[POOL_HALO:, :]
        cnt = jnp.minimum(w, pos + 1).astype(F32)
        d = (acc / cnt - xa[:, sl]).astype(BF16)
        a_parts.append(_dot(d, pw_ref[g]))
    a = jnp.concatenate(a_parts, axis=-1) * ps_ref[...]
    ga = _dot(h, win_ref[:, width:2 * width])
    mix_a = (a * _silu(ga)).astype(BF16)

    u = _dot(h, win_ref[:, 2 * width:3 * width])
    v = _dot(h, win_ref[:, 3 * width:4 * width])
    gbuf[CONV_HALO:CONV_HALO + tt, :] = u * _sigmoid(v)

    first_row = CONV_HALO - CONV_BUF

    def conv_chunk(i, carry):
        r0 = pl.multiple_of(i * CONV_ROWS, CONV_ROWS)
        for lc in range(width // CONV_LANES):
            ls = slice(lc * CONV_LANES, (lc + 1) * CONV_LANES)
            rows = CONV_ROWS + CONV_HALO
            win = gbuf[pl.ds(r0, rows), ls]
            acc = jnp.zeros((CONV_ROWS, CONV_LANES), F32)
            for r in range(SUBLANES):
                taps = [k for k in range(CONV_K) if (first_row + k) % SUBLANES == r]
                if not taps:
                    continue
                shifted = pltpu.roll(win, rows - r, 0) if r else win
                for k in taps:
                    o = (first_row + k) // SUBLANES * SUBLANES
                    acc = acc + shifted[o:o + CONV_ROWS, :] * cw_ref[k:k + 1, ls]
            cbuf[pl.ds(r0, CONV_ROWS), ls] = acc
        return carry

    lax.fori_loop(0, tt // CONV_ROWS, conv_chunk, 0)
    c = _layernorm_silu(cbuf[...] + cb_ref[...], lnw_ref[...], lnb_ref[...])
    gb = _dot(h, win_ref[:, 4 * width:5 * width])
    mix_b = (c * _silu(gb)).astype(BF16)

    y_ref[0] = x + _dot(mix_a, wout_ref[0:width, :]) + _dot(mix_b, wout_ref[width:2 * width, :])

    @pl.when(t == nt - 1)
    def _():
        ptail_ref[0] = xbuf[POOL_HALO + tt - POOL_BUF:POOL_HALO + tt, :]
        ctail_ref[0] = gbuf[CONV_HALO + tt - CONV_BUF:CONV_HALO + tt, :]

    xbuf[0:POOL_HALO, :] = xbuf[tt:tt + POOL_HALO, :]
    gbuf[0:CONV_HALO, :] = gbuf[tt:tt + CONV_HALO, :]


def _ab_prompt(x, nw, win, pw, ps, cw, cb, lnw, lnb, wout, *, tt):
    b, t, d = x.shape
    width = ps.shape[-1]
    kern = functools.partial(_ab_prompt_kernel, tt=tt, width=width)
    tile = pl.BlockSpec((1, tt, d), lambda i, j: (i, j, 0))
    return pl.pallas_call(
        kern,
        grid=(b, t // tt),
        in_specs=[tile, _const_spec(nw.shape), _const_spec(win.shape), _const_spec(pw.shape),
                  _const_spec(ps.shape), _const_spec(cw.shape), _const_spec(cb.shape),
                  _const_spec(lnw.shape), _const_spec(lnb.shape), _const_spec(wout.shape)],
        out_specs=[tile,
                   pl.BlockSpec((1, POOL_BUF, width), lambda i, j: (i, 0, 0)),
                   pl.BlockSpec((1, CONV_BUF, width), lambda i, j: (i, 0, 0))],
        out_shape=[jax.ShapeDtypeStruct((b, t, d), F32),
                   jax.ShapeDtypeStruct((b, POOL_BUF, width), F32),
                   jax.ShapeDtypeStruct((b, CONV_BUF, width), F32)],
        scratch_shapes=[pltpu.VMEM((POOL_HALO + tt, width), F32),
                        pltpu.VMEM((CONV_HALO + tt, width), F32),
                        pltpu.VMEM((tt, width), F32)],
        compiler_params=_params(("arbitrary", "arbitrary")),
        name="ab_prompt",
    )(x, nw, win, pw, ps, cw, cb, lnw, lnb, wout)


def _ab_sample_kernel(x_ref, sp_ref, sc_ref, nw_ref, win_ref, pw_ref, ps_ref, cw_ref, cb_ref, lnw_ref, lnb_ref,
                      wout_ref, y_ref, xa_ref, glu_ref, *, tn, nb, width):
    group = width // len(POOL_WINDOWS)
    x = x_ref[...].reshape(tn * nb, x_ref.shape[-1])
    h = _rms_rows(x, nw_ref[...]).astype(BF16)

    xa = _dot(h, win_ref[:, 0:width])
    xa_ref[...] = xa.reshape(tn, nb, width)

    def pool_row(j):
        return sp_ref[j] if j < POOL_BUF else xa[(j - POOL_BUF) * nb:(j - POOL_BUF + 1) * nb, :]

    d_rows = []
    for ti in range(tn):
        parts = []
        for g, w in enumerate(POOL_WINDOWS):
            sl = slice(g * group, (g + 1) * group)
            acc = pool_row(POOL_BUF + ti)[:, sl]
            for j in range(1, w):
                acc = acc + pool_row(POOL_BUF + ti - j)[:, sl]
            cnt = float(min(w, ti + 1 + POOL_BUF))
            parts.append(acc / cnt - pool_row(POOL_BUF + ti)[:, sl])
        d_rows.append(jnp.concatenate(parts, axis=-1))
    d = jnp.concatenate(d_rows, axis=0).astype(BF16)
    a = jnp.concatenate([_dot(d[:, g * group:(g + 1) * group], pw_ref[g]) for g in range(len(POOL_WINDOWS))],
                        axis=-1) * ps_ref[...]
    ga = _dot(h, win_ref[:, width:2 * width])
    mix_a = (a * _silu(ga)).astype(BF16)

    u = _dot(h, win_ref[:, 2 * width:3 * width])
    v = _dot(h, win_ref[:, 3 * width:4 * width])
    glu = u * _sigmoid(v)
    glu_ref[...] = glu.reshape(tn, nb, width)

    def conv_row(j):
        return sc_ref[j] if j < CONV_BUF else glu[(j - CONV_BUF) * nb:(j - CONV_BUF + 1) * nb, :]

    c_rows = []
    for ti in range(tn):
        acc = jnp.zeros((nb, width), F32)
        for k in range(CONV_K):
            acc = acc + conv_row(ti + k) * cw_ref[k:k + 1, :]
        c_rows.append(acc)
    c = _layernorm_silu(jnp.concatenate(c_rows, axis=0) + cb_ref[...], lnw_ref[...], lnb_ref[...])
    gb = _dot(h, win_ref[:, 4 * width:5 * width])
    mix_b = (c * _silu(gb)).astype(BF16)

    y = x + _dot(mix_a, wout_ref[0:width, :]) + _dot(mix_b, wout_ref[width:2 * width, :])
    y_ref[...] = y.reshape(tn, nb, y_ref.shape[-1])


def _ab_sample(x_t, sp_t, sc_t, nw, win, pw, ps, cw, cb, lnw, lnb, wout):
    tn, nb, d = x_t.shape
    width = ps.shape[-1]
    kern = functools.partial(_ab_sample_kernel, tn=tn, nb=nb, width=width)
    args = (x_t, sp_t, sc_t, nw, win, pw, ps, cw, cb, lnw, lnb, wout)
    return pl.pallas_call(
        kern,
        grid=(1,),
        in_specs=[_const_spec(a.shape) for a in args],
        out_specs=[_const_spec((tn, nb, d)), _const_spec((tn, nb, width)), _const_spec((tn, nb, width))],
        out_shape=[jax.ShapeDtypeStruct((tn, nb, d), F32),
                   jax.ShapeDtypeStruct((tn, nb, width), F32),
                   jax.ShapeDtypeStruct((tn, nb, width), F32)],
        compiler_params=_params(("arbitrary",)),
        name="ab_sample",
    )(*args)


def _subhead_norm(x, w2):
    lane = lax.broadcasted_iota(jnp.int32, x.shape, 1)
    lo = lane < HEAD_DIM
    sq = x * x
    s_lo = jnp.sum(jnp.where(lo, sq, 0.0), axis=-1, keepdims=True)
    s_hi = jnp.sum(jnp.where(lo, 0.0, sq), axis=-1, keepdims=True)
    ms = jnp.where(lo, s_lo, s_hi) * (1.0 / HEAD_DIM)
    return x * lax.rsqrt(ms + EPS) * w2


def _qkv_kernel(x_ref, nw_ref, win_ref, qw_ref, kw_ref, *out_refs, cw, head_major):
    x = x_ref[...]
    h = _rms_rows(x, nw_ref[...]).astype(BF16)
    q = _dot(h, win_ref[:, 0:cw])
    k = _dot(h, win_ref[:, cw:2 * cw])
    v = _dot(h, win_ref[:, 2 * cw:3 * cw])
    g = _dot(h, win_ref[:, 3 * cw:4 * cw])
    if head_major:
        k_ref, v_ref, g_ref, q1_ref, q2_ref, kb_ref, vb_ref = out_refs
    else:
        k_ref, v_ref, g_ref, q_ref = out_refs
    v_ref[...] = v
    g_ref[...] = g
    for hd in range(N_HEADS):
        sl = slice(hd * V_DIM, (hd + 1) * V_DIM)
        qn = _subhead_norm(q[:, sl], qw_ref[...]) * (ATTN_SCALE * LOG2E)
        kn = _subhead_norm(k[:, sl], kw_ref[...])
        k_ref[:, sl] = kn
        if head_major:
            qt = qn.T
            lo = lax.broadcasted_iota(jnp.int32, qt.shape, 0) < HEAD_DIM
            q1_ref[hd, 0] = jnp.where(lo, qt, 0.0).astype(BF16)
            q2_ref[hd, 0] = jnp.where(lo, 0.0, qt).astype(BF16)
            kb_ref[hd] = kn.astype(BF16)
            vb_ref[hd, 0] = v[:, sl].T.astype(BF16)
        else:
            q_ref[:, sl] = qn


def _qkv(x2d, nw, win, qw2, kw2, *, tt, head_major, attn_block=None):
    n, d = x2d.shape
    cw = win.shape[1] // 4
    tile = pl.BlockSpec((tt, d), lambda i: (i, 0))
    wide = pl.BlockSpec((tt, cw), lambda i: (i, 0))
    out_specs = [wide, wide, wide]
    out_shape = [jax.ShapeDtypeStruct((n, cw), F32)] * 3
    if head_major:
        per = attn_block // tt
        hm = pl.BlockSpec((N_HEADS, tt, V_DIM), lambda i: (0, i, 0))
        hmt = pl.BlockSpec((N_HEADS, 1, V_DIM, tt), lambda i: (0, i // per, 0, i % per))
        out_specs += [hmt, hmt, hm, hmt]
        tshape = jax.ShapeDtypeStruct((N_HEADS, n // attn_block, V_DIM, attn_block), BF16)
        out_shape += [tshape, tshape, jax.ShapeDtypeStruct((N_HEADS, n, V_DIM), BF16), tshape]
    else:
        out_specs += [wide]
        out_shape += [jax.ShapeDtypeStruct((n, cw), F32)]
    return pl.pallas_call(
        functools.partial(_qkv_kernel, cw=cw, head_major=head_major),
        grid=(n // tt,),
        in_specs=[tile, _const_spec(nw.shape), _const_spec(win.shape), _const_spec(qw2.shape),
                  _const_spec(kw2.shape)],
        out_specs=out_specs,
        out_shape=out_shape,
        compiler_params=_params(("arbitrary",)),
        name="qkv_prompt" if head_major else "qkv_sample",
    )(x2d, nw, win, qw2, kw2)


def _lambda(lq1_ref, lk1_ref, lq2_ref, lk2_ref, lambda_init):
    a = jnp.sum(lq1_ref[...] * lk1_ref[...], axis=-1, keepdims=True)
    b = jnp.sum(lq2_ref[...] * lk2_ref[...], axis=-1, keepdims=True)
    return jnp.exp(a) - jnp.exp(b) + lambda_init


def _fold_sublanes(x, op):
    shift = SUBLANES // 2
    while shift:
        x = op(x, pltpu.roll(x, shift, 0))
        shift //= 2
    return x


def _col_reduce(x3, op, reduce_leading):
    return _fold_sublanes(reduce_leading(x3, axis=0), op)


def _prompt_attn_step(qi, q1_ref, q2_ref, k_ref, v_ref, lam, sw_ref, o_ref, m1, l1, a1, m2, l2, a2, *,
                      tq, lambda_init, co_scores, co_finish):
    q1 = q1_ref[0, 0]
    q2 = q2_ref[0, 0]
    tkp = tq // KV_SPLIT
    nd = V_DIM // SUBLANES

    for m, l, a in ((m1, l1, a1), (m2, l2, a2)):
        m[...] = jnp.full(m.shape, NEG_INF, F32)
        l[...] = jnp.zeros(l.shape, F32)
        a[...] = jnp.zeros(a.shape, F32)

    def scores(j):
        chains = []
        ones = jnp.ones((2 * SUBLANES, tkp), BF16)
        for part in range(KV_SPLIT):
            lo = part * tkp
            k = k_ref[0, pl.ds(pl.multiple_of(j * tq, tq) + lo, tkp), :]
            vt = jnp.concatenate([v_ref[0, j, :, lo:lo + tkp], ones], axis=0)
            for q, state in ((q1, (m1, l1, a1)), (q2, (m2, l2, a2))):
                chains.append((_dot(k, q), vt, lo, state))
        return chains

    def softmax_pv(chains, masked):
        for s, vt, lo, (m, l, a) in chains:
            if masked:
                key = lo + lax.broadcasted_iota(jnp.int32, s.shape, 0)
                qry = lax.broadcasted_iota(jnp.int32, s.shape, 1)
                s = jnp.where(key <= qry, s, NEG_INF)
            s3 = s.reshape(tkp // SUBLANES, SUBLANES, tq)
            m_old = m[...]
            m_new = jnp.maximum(m_old, _col_reduce(s3, jnp.maximum, jnp.max))
            p = jnp.exp2(s3 - m_new[None]).reshape(tkp, tq).astype(BF16)
            alpha = jnp.exp2(m_old - m_new)
            pv = _dot(vt, p)
            l[...] = alpha * l[...] + pv[V_DIM:V_DIM + SUBLANES, :]
            a[...] = (alpha[None] * a[...].reshape(nd, SUBLANES, tq)).reshape(V_DIM, tq) + pv[0:V_DIM, :]
            m[...] = m_new

    def body(j, carry):
        softmax_pv(scores(j), False)
        return carry

    lax.fori_loop(0, qi, body, 0)
    chains = scores(qi)
    co = co_scores()
    softmax_pv(chains, True)
    co_finish(co)

    o3 = (a1[...].reshape(nd, SUBLANES, tq) * (1.0 / l1[...])[None]
          - lam * (a2[...].reshape(nd, SUBLANES, tq) * (1.0 / l2[...])[None]))
    ms = _col_reduce(o3 * o3, jnp.add, jnp.sum) * (1.0 / V_DIM)
    on = (o3 * lax.rsqrt(ms + EPS)[None]).reshape(V_DIM, tq)
    o_ref[...] = on.T * sw_ref[...] * (1.0 - lambda_init)


def _sample_attn_step(step, last, q_ref, kn_ref, vn_ref, k_refs, v_refs, lam, sw_ref, o_ref,
                      wq, newk, newv, m_ref, l_ref, acc, *, tn, lambda_init):
    half = tn * N_HEADS
    page_rows = PAGE * N_HEADS

    def same_head(shape):
        row = lax.broadcasted_iota(jnp.int32, shape, 0)
        col = lax.broadcasted_iota(jnp.int32, shape, 1)
        return (col & (N_HEADS - 1)) == (row & (N_HEADS - 1))

    def update(s_list, v_list):
        m_old = m_ref[...]
        m_new = m_old
        for s in s_list:
            m_new = jnp.maximum(m_new, jnp.max(s, axis=-1, keepdims=True))
        alpha = jnp.exp2(m_old - m_new)
        l_new = alpha * l_ref[...]
        a_new = alpha * acc[...]
        for s, v in zip(s_list, v_list):
            p = jnp.exp2(s - m_new)
            l_new = l_new + jnp.sum(p, axis=-1, keepdims=True)
            a_new = a_new + _dot(p.astype(BF16), v)
        m_ref[...] = m_new
        l_ref[...] = l_new
        acc[...] = a_new

    def start():
        @pl.when(step == 0)
        def _():
            qb = q_ref[0]
            lo = lax.broadcasted_iota(jnp.int32, qb.shape, 1) < HEAD_DIM
            wq[...] = jnp.concatenate([jnp.where(lo, qb, 0.0), jnp.where(lo, 0.0, qb)], axis=0).astype(BF16)
            newk[...] = jnp.zeros(newk.shape, F32)
            newv[...] = jnp.zeros(newv.shape, F32)
            newk[0:half, :] = kn_ref[0]
            newv[0:half, :] = vn_ref[0]
            m_ref[...] = jnp.full(m_ref.shape, NEG_INF, F32)
            l_ref[...] = jnp.zeros(l_ref.shape, F32)
            acc[...] = jnp.zeros(acc.shape, F32)
            s = _dot_nt(wq[...], newk[...].astype(BF16))
            row = lax.broadcasted_iota(jnp.int32, s.shape, 0)
            col = lax.broadcasted_iota(jnp.int32, s.shape, 1)
            causal = (col >> 3) <= ((row >> 3) & (tn - 1))
            s = jnp.where(same_head(s.shape), jnp.where(causal, s, NEG_INF), NEG_INF)
            update([s], [newv[...].astype(BF16)])

    def scores():
        w = wq[...]
        return [_dot_nt(w, kr[...].reshape(page_rows, V_DIM).astype(BF16)) for kr in k_refs]

    def finish(raw):
        mask = same_head((2 * half, page_rows))
        update([jnp.where(mask, s, NEG_INF) for s in raw],
               [vr[...].reshape(page_rows, V_DIM).astype(BF16) for vr in v_refs])

        @pl.when(step == last)
        def _():
            o = acc[...] / l_ref[...]
            od = o[0:half, :] - lam * o[half:2 * half, :]
            o_ref[0] = _rms_rows(od, sw_ref[...]) * (1.0 - lambda_init)

    return start, scores, finish


def _attn_kernel(pt_ref, q1_ref, q2_ref, k_ref, v_ref, lq1_ref, lk1_ref, lq2_ref, lk2_ref, sw_ref,
                 dq_ref, dkn_ref, dvn_ref, *refs, pages, chunks, tq, tn, lambda_init):
    k_refs = refs[:pages]
    v_refs = refs[pages:2 * pages]
    o_ref, do_ref = refs[2 * pages:2 * pages + 2]
    prompt_state = refs[2 * pages + 2:2 * pages + 8]
    sample_state = refs[2 * pages + 8:]
    lam = _lambda(lq1_ref, lk1_ref, lq2_ref, lk2_ref, lambda_init)
    qi = pl.program_id(2)
    lin = (pl.program_id(0) * pl.num_programs(1) + pl.program_id(1)) * pl.num_programs(2) + qi
    start, scores, finish = _sample_attn_step(lin % chunks, chunks - 1, dq_ref, dkn_ref, dvn_ref, k_refs, v_refs,
                                              lam, sw_ref, do_ref, *sample_state, tn=tn, lambda_init=lambda_init)
    start()
    _prompt_attn_step(qi, q1_ref, q2_ref, k_ref, v_ref, lam, sw_ref, o_ref, *prompt_state,
                      tq=tq, lambda_init=lambda_init, co_scores=scores, co_finish=finish)


def _attn(page_table, q1, q2, kb, vb, dq, dk_new, dv_new, cache_k, cache_v, lq1, lk1, lq2, lk2, sw, *,
          nb, t, tq, layer, pages, lambda_init):
    nq = t // tq
    nd, half, _ = dq.shape
    tn = half // N_HEADS
    n_pages = page_table.shape[1]
    chunks = n_pages // pages
    assert nb * N_HEADS * nq == nd * chunks, "one chunk of sample pages per prompt grid step"

    def lin(b, h, i):
        return (b * N_HEADS + h) * nq + i

    qspec = pl.BlockSpec((1, 1, V_DIM, tq), lambda b, h, i, pt: (h, b * nq + i, 0, 0))
    kspec = pl.BlockSpec((1, t, V_DIM), lambda b, h, i, pt: (h, b, 0))
    vspec = pl.BlockSpec((1, nq, V_DIM, tq), lambda b, h, i, pt: (h, b, 0, 0))
    vec = pl.BlockSpec(lq1.shape, lambda b, h, i, pt: (0, 0))
    tok = pl.BlockSpec((1, half, V_DIM), lambda b, h, i, pt: (lin(b, h, i) // chunks, 0, 0))

    def page_spec(p):
        def index(b, h, i, pt):
            step = lin(b, h, i)
            return (layer, pt[(step // chunks) * n_pages + (step % chunks) * pages + p], 0, 0, 0)
        return pl.BlockSpec((None, None, PAGE, N_HEADS, V_DIM), index)

    rows = 2 * half
    grid_spec = pltpu.PrefetchScalarGridSpec(
        num_scalar_prefetch=1,
        grid=(nb, N_HEADS, nq),
        in_specs=[qspec, qspec, kspec, vspec, vec, vec, vec, vec,
                  pl.BlockSpec(sw.shape, lambda b, h, i, pt: (0, 0)), tok, tok, tok]
                 + [page_spec(p) for p in range(pages)] * 2,
        out_specs=[pl.BlockSpec((tq, V_DIM), lambda b, h, i, pt: (b * nq + i, h)), tok],
        scratch_shapes=[pltpu.VMEM((SUBLANES, tq), F32), pltpu.VMEM((SUBLANES, tq), F32),
                        pltpu.VMEM((V_DIM, tq), F32)] * 2
                       + [pltpu.VMEM((rows, V_DIM), BF16),
                          pltpu.VMEM((LANES, V_DIM), F32), pltpu.VMEM((LANES, V_DIM), F32),
                          pltpu.VMEM((rows, 1), F32), pltpu.VMEM((rows, 1), F32),
                          pltpu.VMEM((rows, V_DIM), F32)],
    )
    return pl.pallas_call(
        functools.partial(_attn_kernel, pages=pages, chunks=chunks, tq=tq, tn=tn, lambda_init=lambda_init),
        grid_spec=grid_spec,
        out_shape=[jax.ShapeDtypeStruct((nb * t, N_HEADS * V_DIM), F32),
                   jax.ShapeDtypeStruct((nd, half, V_DIM), F32)],
        compiler_params=_params(("arbitrary", "arbitrary", "arbitrary")),
        name="attn",
    )(page_table.reshape(-1), q1, q2, kb, vb, lq1, lk1, lq2, lk2, sw, dq, dk_new, dv_new,
      *([cache_k] * pages), *([cache_v] * pages))


def _out_kernel(x_ref, o_ref, g_ref, w_ref, y_ref):
    gated = (o_ref[...] * _silu(g_ref[...])).astype(BF16)
    y_ref[...] = x_ref[...] + _dot(gated, w_ref[...])


def _out_proj(x2d, o, g, w, *, tt):
    n, d = x2d.shape
    tile = pl.BlockSpec((tt, d), lambda i: (i, 0))
    wide = pl.BlockSpec((tt, w.shape[0]), lambda i: (i, 0))
    return pl.pallas_call(
        _out_kernel,
        grid=(n // tt,),
        in_specs=[tile, wide, wide, _const_spec(w.shape)],
        out_specs=tile,
        out_shape=jax.ShapeDtypeStruct((n, d), F32),
        compiler_params=_params(("arbitrary",)),
        name="out_proj",
    )(x2d, o, g, w)


def kernel(x_prompt, x_sample, state_pool, state_conv, cache_k, cache_v, page_table, norm_w_ab, w_in_ab, pool_w, pool_scale, conv_w, conv_b, conv_ln_w, conv_ln_b, w_out_ab, norm_w_c, w_in_c, q_norm_w, k_norm_w, lambda_q1, lambda_k1, lambda_q2, lambda_k2, subln_w, w_out_c):
    nb, t, d = x_prompt.shape
    nd, tn, _ = x_sample.shape
    depth = norm_w_ab.shape[0] + norm_w_c.shape[0]
    row = lambda a: a.reshape(1, -1)

    yp, ys = x_prompt, x_sample
    pool_p, conv_p, k_p, v_p = [], [], [], []
    pool_s, conv_s, k_s, v_s = [], [], [], []
    for l in range(depth):
        j = l // 2
        if l % 2 == 0:
            weights = (row(norm_w_ab[j]), w_in_ab[j].astype(BF16), pool_w[j].astype(BF16), row(pool_scale[j]),
                       conv_w[j], row(conv_b[j]), row(conv_ln_w[j]), row(conv_ln_b[j]), w_out_ab[j].astype(BF16))
            yp, ptail, ctail = _ab_prompt(yp, *weights, tt=256)
            pool_p.append(ptail)
            conv_p.append(ctail)
            ys_t, xa_t, glu_t = _ab_sample(ys.transpose(1, 0, 2), state_pool[j].transpose(1, 0, 2),
                                           state_conv[j].transpose(1, 0, 2), *weights)
            ys = ys_t.transpose(1, 0, 2)
            pool_s.append(jnp.concatenate([state_pool[j], xa_t.transpose(1, 0, 2)], axis=1)[:, -POOL_BUF:])
            conv_s.append(jnp.concatenate([state_conv[j], glu_t.transpose(1, 0, 2)], axis=1)[:, -CONV_BUF:])
        else:
            lambda_init = 0.8 - 0.6 * math.exp(-0.3 * l)
            nw = row(norm_w_c[j])
            win = w_in_c[j].astype(BF16)
            wout = w_out_c[j].astype(BF16)
            qw2 = row(jnp.tile(q_norm_w[j], 2))
            kw2 = row(jnp.tile(k_norm_w[j], 2))
            lams = (row(lambda_q1[j]), row(lambda_k1[j]), row(lambda_q2[j]), row(lambda_k2[j]))
            sw = row(subln_w[j])
            cw = win.shape[1] // 4
            yp2 = yp.reshape(nb * t, d)
            ys2 = ys.reshape(nd * tn, d)
            tq = 512
            kp, vp, gp, q1, q2, kb, vb = _qkv(yp2, nw, win, qw2, kw2, tt=256, head_major=True, attn_block=tq)
            ks, vs, gs, qs = _qkv(ys2, nw, win, qw2, kw2, tt=nd * tn, head_major=False)
            rows_th = lambda a: a.reshape(nd, tn * N_HEADS, V_DIM)
            op, os_ = _attn(page_table, q1, q2, kb, vb, rows_th(qs), rows_th(ks), rows_th(vs), cache_k, cache_v,
                            *lams, sw, nb=nb, t=t, tq=tq, layer=j, pages=8, lambda_init=lambda_init)
            yp = _out_proj(yp2, op, gp, wout, tt=512).reshape(nb, t, d)
            ys = _out_proj(ys2, os_.reshape(nd * tn, cw), gs, wout, tt=nd * tn).reshape(nd, tn, d)
            k_p.append(kp.reshape(nb, t, N_HEADS, V_DIM))
            v_p.append(vp.reshape(nb, t, N_HEADS, V_DIM))
            k_s.append(ks.reshape(nd, tn, N_HEADS, V_DIM))
            v_s.append(vs.reshape(nd, tn, N_HEADS, V_DIM))
    return (yp, ys, jnp.stack(pool_p), jnp.stack(conv_p), jnp.stack(k_p), jnp.stack(v_p),
            jnp.stack(pool_s), jnp.stack(conv_s), jnp.stack(k_s), jnp.stack(v_s))
```

```python
import functools
import math

import jax
import jax.numpy as jnp
from jax import lax
from jax.experimental import pallas as pl
from jax.experimental.pallas import tpu as pltpu

F32 = jnp.float32
BF16 = jnp.bfloat16

POOL_WINDOWS = (2, 4, 8, 16)
POOL_BUF = max(POOL_WINDOWS) - 1
assert all(w & (w - 1) == 0 for w in POOL_WINDOWS)
CONV_K = 31
CONV_BUF = CONV_K - 1
N_HEADS = 8
HEAD_DIM = 64
V_DIM = 2 * HEAD_DIM
PAGE = 128
EPS = 1e-6
NEG_INF = -1e30
ATTN_SCALE = HEAD_DIM ** -0.5
LOG2E = math.log2(math.e)

SUBLANES = 8
LANES = 128
VMEM_LIMIT = 56 * 1024 * 1024

POOL_HALO = 16
CONV_HALO = 32
CONV_ROWS = 64
CONV_LANES = 128
KV_SPLIT = 2
LOOP_BLOCKS = 2


def _sigmoid(x):
    return 1.0 / (1.0 + jnp.exp(-x))


def _silu(x):
    return x * _sigmoid(x)


def _rms_rows(x, w):
    ms = jnp.mean(x * x, axis=-1, keepdims=True)
    return x * lax.rsqrt(ms + EPS) * w


def _dot(a, b):
    return jnp.dot(a, b, preferred_element_type=F32)


def _dot_nt(a, b):
    return lax.dot_general(a, b, (((1,), (1,)), ((), ())), preferred_element_type=F32)


def _const_spec(shape):
    n = len(shape)
    return pl.BlockSpec(shape, lambda *_: (0,) * n)


def _params(semantics):
    return pltpu.CompilerParams(dimension_semantics=semantics, vmem_limit_bytes=VMEM_LIMIT)


def _layernorm_silu(c, ln_w, ln_b):
    mu = jnp.mean(c, axis=-1, keepdims=True)
    xc = c - mu
    var = jnp.mean(xc * xc, axis=-1, keepdims=True)
    return _silu(xc * lax.rsqrt(var + EPS) * ln_w + ln_b)


def _ab_prompt_kernel(x_ref, nw_ref, win_ref, pw_ref, ps_ref, cw_ref, cb_ref, lnw_ref, lnb_ref, wout_ref,
                      y_ref, ptail_ref, ctail_ref, xbuf, gbuf, cbuf, *, tt, width):
    t = pl.program_id(1)
    nt = pl.num_programs(1)
    group = width // len(POOL_WINDOWS)

    @pl.when(t == 0)
    def _():
        xbuf[0:POOL_HALO, :] = jnp.zeros((POOL_HALO, width), F32)
        gbuf[0:CONV_HALO, :] = jnp.zeros((CONV_HALO, width), F32)

    x = x_ref[0]
    h = _rms_rows(x, nw_ref[...]).astype(BF16)

    u = _dot(h, win_ref[:, 2 * width:3 * width])
    v = _dot(h, win_ref[:, 3 * width:4 * width])
    gbuf[CONV_HALO:CONV_HALO + tt, :] = u * _sigmoid(v)

    xa = _dot(h, win_ref[:, 0:width])
    xbuf[POOL_HALO:POOL_HALO + tt, :] = xa
    pos = t * tt + lax.broadcasted_iota(jnp.int32, (tt, 1), 0)
    a_parts = []
    for g, w in enumerate(POOL_WINDOWS):
        sl = slice(g * group, (g + 1) * group)
        acc = xbuf[:, sl]
        j = 1
        while j < w:
            acc = acc + pltpu.roll(acc, j, 0)
            j *= 2
        acc = acc[POOL_HALO:, :]
        cnt = jnp.minimum(w, pos + 1).astype(F32)
        d = (acc / cnt - xa[:, sl]).astype(BF16)
        a_parts.append(_dot(d, pw_ref[g]))
    a = jnp.concatenate(a_parts, axis=-1) * ps_ref[...]
    ga = _dot(h, win_ref[:, width:2 * width])
    mix_a = (a * _silu(ga)).astype(BF16)
    gb = _dot(h, win_ref[:, 4 * width:5 * width])
    y_a = x + _dot(mix_a, wout_ref[0:width, :])

    first_row = CONV_HALO - CONV_BUF
    rows = CONV_ROWS + CONV_HALO
    for r0 in range(0, tt, CONV_ROWS):
        for lc in range(width // CONV_LANES):
            ls = slice(lc * CONV_LANES, (lc + 1) * CONV_LANES)
            win = gbuf[r0:r0 + rows, ls]
            acc = jnp.zeros((CONV_ROWS, CONV_LANES), F32)
            for r in range(SUBLANES):
                taps = [k for k in range(CONV_K) if (first_row + k) % SUBLANES == r]
                if not taps:
                    continue
                shifted = pltpu.roll(win, rows - r, 0) if r else win
                for k in taps:
                    o = (first_row + k) // SUBLANES * SUBLANES
                    acc = acc + shifted[o:o + CONV_ROWS, :] * cw_ref[k:k + 1, ls]
            cbuf[r0:r0 + CONV_ROWS, ls] = acc
    c = _layernorm_silu(cbuf[...] + cb_ref[...], lnw_ref[...], lnb_ref[...])
    mix_b = (c * _silu(gb)).astype(BF16)

    y_ref[0] = y_a + _dot(mix_b, wout_ref[width:2 * width, :])

    @pl.when(t == nt - 1)
    def _():
        ptail_ref[0] = xbuf[POOL_HALO + tt - POOL_BUF:POOL_HALO + tt, :]
        ctail_ref[0] = gbuf[CONV_HALO + tt - CONV_BUF:CONV_HALO + tt, :]

    xbuf[0:POOL_HALO, :] = xbuf[tt:tt + POOL_HALO, :]
    gbuf[0:CONV_HALO, :] = gbuf[tt:tt + CONV_HALO, :]


def _ab_prompt(x, nw, win, pw, ps, cw, cb, lnw, lnb, wout, *, tt):
    b, t, d = x.shape
    width = ps.shape[-1]
    kern = functools.partial(_ab_prompt_kernel, tt=tt, width=width)
    tile = pl.BlockSpec((1, tt, d), lambda i, j: (i, j, 0))
    return pl.pallas_call(
        kern,
        grid=(b, t // tt),
        in_specs=[tile, _const_spec(nw.shape), _const_spec(win.shape), _const_spec(pw.shape),
                  _const_spec(ps.shape), _const_spec(cw.shape), _const_spec(cb.shape),
                  _const_spec(lnw.shape), _const_spec(lnb.shape), _const_spec(wout.shape)],
        out_specs=[tile,
                   pl.BlockSpec((1, POOL_BUF, width), lambda i, j: (i, 0, 0)),
                   pl.BlockSpec((1, CONV_BUF, width), lambda i, j: (i, 0, 0))],
        out_shape=[jax.ShapeDtypeStruct((b, t, d), F32),
                   jax.ShapeDtypeStruct((b, POOL_BUF, width), F32),
                   jax.ShapeDtypeStruct((b, CONV_BUF, width), F32)],
        scratch_shapes=[pltpu.VMEM((POOL_HALO + tt, width), F32),
                        pltpu.VMEM((CONV_HALO + tt, width), F32),
                        pltpu.VMEM((tt, width), F32)],
        compiler_params=_params(("arbitrary", "arbitrary")),
        name="ab_prompt",
    )(x, nw, win, pw, ps, cw, cb, lnw, lnb, wout)


def _ab_sample_kernel(x_ref, sp_ref, sc_ref, nw_ref, win_ref, pw_ref, ps_ref, cw_ref, cb_ref, lnw_ref, lnb_ref,
                      wout_ref, y_ref, xa_ref, glu_ref, *, tn, nb, width):
    group = width // len(POOL_WINDOWS)
    x = x_ref[...].reshape(tn * nb, x_ref.shape[-1])
    h = _rms_rows(x, nw_ref[...]).astype(BF16)

    xa = _dot(h, win_ref[:, 0:width])
    xa_ref[...] = xa.reshape(tn, nb, width)

    def pool_row(j):
        return sp_ref[j] if j < POOL_BUF else xa[(j - POOL_BUF) * nb:(j - POOL_BUF + 1) * nb, :]

    d_rows = []
    for ti in range(tn):
        parts = []
        for g, w in enumerate(POOL_WINDOWS):
            sl = slice(g * group, (g + 1) * group)
            acc = pool_row(POOL_BUF + ti)[:, sl]
            for j in range(1, w):
                acc = acc + pool_row(POOL_BUF + ti - j)[:, sl]
            cnt = float(min(w, ti + 1 + POOL_BUF))
            parts.append(acc / cnt - pool_row(POOL_BUF + ti)[:, sl])
        d_rows.append(jnp.concatenate(parts, axis=-1))
    d = jnp.concatenate(d_rows, axis=0).astype(BF16)
    a = jnp.concatenate([_dot(d[:, g * group:(g + 1) * group], pw_ref[g]) for g in range(len(POOL_WINDOWS))],
                        axis=-1) * ps_ref[...]
    ga = _dot(h, win_ref[:, width:2 * width])
    mix_a = (a * _silu(ga)).astype(BF16)

    u = _dot(h, win_ref[:, 2 * width:3 * width])
    v = _dot(h, win_ref[:, 3 * width:4 * width])
    glu = u * _sigmoid(v)
    glu_ref[...] = glu.reshape(tn, nb, width)

    def conv_row(j):
        return sc_ref[j] if j < CONV_BUF else glu[(j - CONV_BUF) * nb:(j - CONV_BUF + 1) * nb, :]

    c_rows = []
    for ti in range(tn):
        acc = jnp.zeros((nb, width), F32)
        for k in range(CONV_K):
            acc = acc + conv_row(ti + k) * cw_ref[k:k + 1, :]
        c_rows.append(acc)
    c = _layernorm_silu(jnp.concatenate(c_rows, axis=0) + cb_ref[...], lnw_ref[...], lnb_ref[...])
    gb = _dot(h, win_ref[:, 4 * width:5 * width])
    mix_b = (c * _silu(gb)).astype(BF16)

    y = x + _dot(mix_a, wout_ref[0:width, :]) + _dot(mix_b, wout_ref[width:2 * width, :])
    y_ref[...] = y.reshape(tn, nb, y_ref.shape[-1])


def _ab_sample(x_t, sp_t, sc_t, nw, win, pw, ps, cw, cb, lnw, lnb, wout):
    tn, nb, d = x_t.shape
    width = ps.shape[-1]
    kern = functools.partial(_ab_sample_kernel, tn=tn, nb=nb, width=width)
    args = (x_t, sp_t, sc_t, nw, win, pw, ps, cw, cb, lnw, lnb, wout)
    return pl.pallas_call(
        kern,
        grid=(1,),
        in_specs=[_const_spec(a.shape) for a in args],
        out_specs=[_const_spec((tn, nb, d)), _const_spec((tn, nb, width)), _const_spec((tn, nb, width))],
        out_shape=[jax.ShapeDtypeStruct((tn, nb, d), F32),
                   jax.ShapeDtypeStruct((tn, nb, width), F32),
                   jax.ShapeDtypeStruct((tn, nb, width), F32)],
        compiler_params=_params(("arbitrary",)),
        name="ab_sample",
    )(*args)


def _subhead_norm(x, w2):
    lane = lax.broadcasted_iota(jnp.int32, x.shape, 1)
    lo = lane < HEAD_DIM
    sq = x * x
    s_lo = jnp.sum(jnp.where(lo, sq, 0.0), axis=-1, keepdims=True)
    s_hi = jnp.sum(jnp.where(lo, 0.0, sq), axis=-1, keepdims=True)
    ms = jnp.where(lo, s_lo, s_hi) * (1.0 / HEAD_DIM)
    return x * lax.rsqrt(ms + EPS) * w2


def _qkv_kernel(x_ref, nw_ref, win_ref, qw_ref, kw_ref, *out_refs, cw, head_major):
    x = x_ref[...]
    h = _rms_rows(x, nw_ref[...]).astype(BF16)
    q = _dot(h, win_ref[:, 0:cw])
    k = _dot(h, win_ref[:, cw:2 * cw])
    v = _dot(h, win_ref[:, 2 * cw:3 * cw])
    g = _dot(h, win_ref[:, 3 * cw:4 * cw])
    if head_major:
        k_ref, v_ref, g_ref, q1_ref, q2_ref, kb_ref, vb_ref = out_refs
    else:
        k_ref, v_ref, g_ref, q_ref = out_refs
    v_ref[...] = v
    g_ref[...] = g
    for hd in range(N_HEADS):
        sl = slice(hd * V_DIM, (hd + 1) * V_DIM)
        qn = _subhead_norm(q[:, sl], qw_ref[...]) * (ATTN_SCALE * LOG2E)
        kn = _subhead_norm(k[:, sl], kw_ref[...])
        k_ref[:, sl] = kn
        if head_major:
            qt = qn.T
            lo = lax.broadcasted_iota(jnp.int32, qt.shape, 0) < HEAD_DIM
            q1_ref[hd, 0] = jnp.where(lo, qt, 0.0).astype(BF16)
            q2_ref[hd, 0] = jnp.where(lo, 0.0, qt).astype(BF16)
            kb_ref[hd] = kn.astype(BF16)
            vb_ref[hd, 0] = v[:, sl].T.astype(BF16)
        else:
            q_ref[:, sl] = qn


def _qkv(x2d, nw, win, qw2, kw2, *, tt, head_major, attn_block=None):
    n, d = x2d.shape
    cw = win.shape[1] // 4
    tile = pl.BlockSpec((tt, d), lambda i: (i, 0))
    wide = pl.BlockSpec((tt, cw), lambda i: (i, 0))
    out_specs = [wide, wide, wide]
    out_shape = [jax.ShapeDtypeStruct((n, cw), F32)] * 3
    if head_major:
        per = attn_block // tt
        hm = pl.BlockSpec((N_HEADS, tt, V_DIM), lambda i: (0, i, 0))
        hmt = pl.BlockSpec((N_HEADS, 1, V_DIM, tt), lambda i: (0, i // per, 0, i % per))
        out_specs += [hmt, hmt, hm, hmt]
        tshape = jax.ShapeDtypeStruct((N_HEADS, n // attn_block, V_DIM, attn_block), BF16)
        out_shape += [tshape, tshape, jax.ShapeDtypeStruct((N_HEADS, n, V_DIM), BF16), tshape]
    else:
        out_specs += [wide]
        out_shape += [jax.ShapeDtypeStruct((n, cw), F32)]
    return pl.pallas_call(
        functools.partial(_qkv_kernel, cw=cw, head_major=head_major),
        grid=(n // tt,),
        in_specs=[tile, _const_spec(nw.shape), _const_spec(win.shape), _const_spec(qw2.shape),
                  _const_spec(kw2.shape)],
        out_specs=out_specs,
        out_shape=out_shape,
        compiler_params=_params(("arbitrary",)),
        name="qkv_prompt" if head_major else "qkv_sample",
    )(x2d, nw, win, qw2, kw2)


def _lambda(lq1_ref, lk1_ref, lq2_ref, lk2_ref, lambda_init):
    a = jnp.sum(lq1_ref[...] * lk1_ref[...], axis=-1, keepdims=True)
    b = jnp.sum(lq2_ref[...] * lk2_ref[...], axis=-1, keepdims=True)
    return jnp.exp(a) - jnp.exp(b) + lambda_init


def _fold_sublanes(x, op):
    shift = SUBLANES // 2
    while shift:
        x = op(x, pltpu.roll(x, shift, 0))
        shift //= 2
    return x


def _col_reduce(x3, op, reduce_leading):
    return _fold_sublanes(reduce_leading(x3, axis=0), op)


def _prompt_attn_step(qi, q1_ref, q2_ref, k_ref, v_ref, lam, sw_ref, o_ref, m1, l1, a1, m2, l2, a2, *,
                      tq, lambda_init, co_scores, co_finish):
    q1 = q1_ref[0, 0]
    q2 = q2_ref[0, 0]
    tkp = tq // KV_SPLIT
    nd = V_DIM // SUBLANES

    for m, l, a in ((m1, l1, a1), (m2, l2, a2)):
        m[...] = jnp.full(m.shape, NEG_INF, F32)
        l[...] = jnp.zeros(l.shape, F32)
        a[...] = jnp.zeros(a.shape, F32)

    def scores(j):
        chains = []
        ones = jnp.ones((2 * SUBLANES, tkp), BF16)
        for part in range(KV_SPLIT):
            lo = part * tkp
            k = k_ref[0, pl.ds(pl.multiple_of(j * tq, tq) + lo, tkp), :]
            vt = jnp.concatenate([v_ref[0, j, :, lo:lo + tkp], ones], axis=0)
            for q, state in ((q1, (m1, l1, a1)), (q2, (m2, l2, a2))):
                chains.append((_dot(k, q), vt, lo, state))
        return chains

    def softmax_pv(chains, masked):
        for s, vt, lo, (m, l, a) in chains:
            if masked:
                key = lo + lax.broadcasted_iota(jnp.int32, s.shape, 0)
                qry = lax.broadcasted_iota(jnp.int32, s.shape, 1)
                s = jnp.where(key <= qry, s, NEG_INF)
            s3 = s.reshape(tkp // SUBLANES, SUBLANES, tq)
            m_old = m[...]
            m_new = jnp.maximum(m_old, _col_reduce(s3, jnp.maximum, jnp.max))
            p = jnp.exp2(s3 - m_new[None]).reshape(tkp, tq).astype(BF16)
            alpha = jnp.exp2(m_old - m_new)
            pv = _dot(vt, p)
            l[...] = alpha * l[...] + pv[V_DIM:V_DIM + SUBLANES, :]
            a[...] = (alpha[None] * a[...].reshape(nd, SUBLANES, tq)).reshape(V_DIM, tq) + pv[0:V_DIM, :]
            m[...] = m_new

    def body(jj, carry):
        base = jj * LOOP_BLOCKS
        chains = []
        for u in range(LOOP_BLOCKS):
            chains += scores(base + u)
        softmax_pv(chains, False)
        return carry

    n_multi = qi // LOOP_BLOCKS
    lax.fori_loop(0, n_multi, body, 0)
    for u in range(LOOP_BLOCKS - 1):
        @pl.when(n_multi * LOOP_BLOCKS + u < qi)
        def _():
            softmax_pv(scores(n_multi * LOOP_BLOCKS + u), False)

    chains = scores(qi)
    co = co_scores()
    softmax_pv(chains, True)
    co_finish(co)

    o3 = (a1[...].reshape(nd, SUBLANES, tq) * (1.0 / l1[...])[None]
          - lam * (a2[...].reshape(nd, SUBLANES, tq) * (1.0 / l2[...])[None]))
    ms = _col_reduce(o3 * o3, jnp.add, jnp.sum) * (1.0 / V_DIM)
    on = (o3 * lax.rsqrt(ms + EPS)[None]).reshape(V_DIM, tq)
    o_ref[...] = on.T * sw_ref[...] * (1.0 - lambda_init)


def _sample_attn_step(step, last, q_ref, kn_ref, vn_ref, k_refs, v_refs, lam, sw_ref, o_ref,
                      wq, newk, newv, m_ref, l_ref, acc, *, tn, lambda_init):
    half = tn * N_HEADS
    page_rows = PAGE * N_HEADS

    def same_head(shape):
        row = lax.broadcasted_iota(jnp.int32, shape, 0)
        col = lax.broadcasted_iota(jnp.int32, shape, 1)
        return (col & (N_HEADS - 1)) == (row & (N_HEADS - 1))

    def update(s_list, v_list):
        m_old = m_ref[...]
        m_new = m_old
        for s in s_list:
            m_new = jnp.maximum(m_new, jnp.max(s, axis=-1, keepdims=True))
        alpha = jnp.exp2(m_old - m_new)
        l_new = alpha * l_ref[...]
        a_new = alpha * acc[...]
        for s, v in zip(s_list, v_list):
            p = jnp.exp2(s - m_new)
            l_new = l_new + jnp.sum(p, axis=-1, keepdims=True)
            a_new = a_new + _dot(p.astype(BF16), v)
        m_ref[...] = m_new
        l_ref[...] = l_new
        acc[...] = a_new

    def start():
        @pl.when(step == 0)
        def _():
            qb = q_ref[0]
            lo = lax.broadcasted_iota(jnp.int32, qb.shape, 1) < HEAD_DIM
            wq[...] = jnp.concatenate([jnp.where(lo, qb, 0.0), jnp.where(lo, 0.0, qb)], axis=0).astype(BF16)
            newk[...] = jnp.zeros(newk.shape, F32)
            newv[...] = jnp.zeros(newv.shape, F32)
            newk[0:half, :] = kn_ref[0]
            newv[0:half, :] = vn_ref[0]
            m_ref[...] = jnp.full(m_ref.shape, NEG_INF, F32)
            l_ref[...] = jnp.zeros(l_ref.shape, F32)
            acc[...] = jnp.zeros(acc.shape, F32)
            s = _dot_nt(wq[...], newk[...].astype(BF16))
            row = lax.broadcasted_iota(jnp.int32, s.shape, 0)
            col = lax.broadcasted_iota(jnp.int32, s.shape, 1)
            causal = (col >> 3) <= ((row >> 3) & (tn - 1))
            s = jnp.where(same_head(s.shape), jnp.where(causal, s, NEG_INF), NEG_INF)
            update([s], [newv[...].astype(BF16)])

    def scores():
        w = wq[...]
        return [_dot_nt(w, kr[...].reshape(page_rows, V_DIM).astype(BF16)) for kr in k_refs]

    def finish(raw):
        mask = same_head((2 * half, page_rows))
        update([jnp.where(mask, s, NEG_INF) for s in raw],
               [vr[...].reshape(page_rows, V_DIM).astype(BF16) for vr in v_refs])

        @pl.when(step == last)
        def _():
            o = acc[...] / l_ref[...]
            od = o[0:half, :] - lam * o[half:2 * half, :]
            o_ref[0] = _rms_rows(od, sw_ref[...]) * (1.0 - lambda_init)

    return start, scores, finish


def _attn_kernel(pt_ref, q1_ref, q2_ref, k_ref, v_ref, lq1_ref, lk1_ref, lq2_ref, lk2_ref, sw_ref,
                 dq_ref, dkn_ref, dvn_ref, *refs, pages, chunks, tq, tn, lambda_init):
    k_refs = refs[:pages]
    v_refs = refs[pages:2 * pages]
    o_ref, do_ref = refs[2 * pages:2 * pages + 2]
    prompt_state = refs[2 * pages + 2:2 * pages + 8]
    sample_state = refs[2 * pages + 8:]
    lam = _lambda(lq1_ref, lk1_ref, lq2_ref, lk2_ref, lambda_init)
    qi = pl.program_id(2)
    lin = (pl.program_id(0) * pl.num_programs(1) + pl.program_id(1)) * pl.num_programs(2) + qi
    start, scores, finish = _sample_attn_step(lin % chunks, chunks - 1, dq_ref, dkn_ref, dvn_ref, k_refs, v_refs,
                                              lam, sw_ref, do_ref, *sample_state, tn=tn, lambda_init=lambda_init)
    start()
    _prompt_attn_step(qi, q1_ref, q2_ref, k_ref, v_ref, lam, sw_ref, o_ref, *prompt_state,
                      tq=tq, lambda_init=lambda_init, co_scores=scores, co_finish=finish)


def _attn(page_table, q1, q2, kb, vb, dq, dk_new, dv_new, cache_k, cache_v, lq1, lk1, lq2, lk2, sw, *,
          nb, t, tq, layer, pages, lambda_init):
    nq = t // tq
    nd, half, _ = dq.shape
    tn = half // N_HEADS
    n_pages = page_table.shape[1]
    chunks = n_pages // pages
    assert nb * N_HEADS * nq == nd * chunks, "one chunk of sample pages per prompt grid step"

    def lin(b, h, i):
        return (b * N_HEADS + h) * nq + i

    qspec = pl.BlockSpec((1, 1, V_DIM, tq), lambda b, h, i, pt: (h, b * nq + i, 0, 0))
    kspec = pl.BlockSpec((1, t, V_DIM), lambda b, h, i, pt: (h, b, 0))
    vspec = pl.BlockSpec((1, nq, V_DIM, tq), lambda b, h, i, pt: (h, b, 0, 0))
    vec = pl.BlockSpec(lq1.shape, lambda b, h, i, pt: (0, 0))
    tok = pl.BlockSpec((1, half, V_DIM), lambda b, h, i, pt: (lin(b, h, i) // chunks, 0, 0))

    def page_spec(p):
        def index(b, h, i, pt):
            step = lin(b, h, i)
            return (layer, pt[(step // chunks) * n_pages + (step % chunks) * pages + p], 0, 0, 0)
        return pl.BlockSpec((None, None, PAGE, N_HEADS, V_DIM), index)

    rows = 2 * half
    grid_spec = pltpu.PrefetchScalarGridSpec(
        num_scalar_prefetch=1,
        grid=(nb, N_HEADS, nq),
        in_specs=[qspec, qspec, kspec, vspec, vec, vec, vec, vec,
                  pl.BlockSpec(sw.shape, lambda b, h, i, pt: (0, 0)), tok, tok, tok]
                 + [page_spec(p) for p in range(pages)] * 2,
        out_specs=[pl.BlockSpec((tq, V_DIM), lambda b, h, i, pt: (b * nq + i, h)), tok],
        scratch_shapes=[pltpu.VMEM((SUBLANES, tq), F32), pltpu.VMEM((SUBLANES, tq), F32),
                        pltpu.VMEM((V_DIM, tq), F32)] * 2
                       + [pltpu.VMEM((rows, V_DIM), BF16),
                          pltpu.VMEM((LANES, V_DIM), F32), pltpu.VMEM((LANES, V_DIM), F32),
                          pltpu.VMEM((rows, 1), F32), pltpu.VMEM((rows, 1), F32),
                          pltpu.VMEM((rows, V_DIM), F32)],
    )
    return pl.pallas_call(
        functools.partial(_attn_kernel, pages=pages, chunks=chunks, tq=tq, tn=tn, lambda_init=lambda_init),
        grid_spec=grid_spec,
        out_shape=[jax.ShapeDtypeStruct((nb * t, N_HEADS * V_DIM), F32),
                   jax.ShapeDtypeStruct((nd, half, V_DIM), F32)],
        compiler_params=_params(("arbitrary", "arbitrary", "arbitrary")),
        name="attn",
    )(page_table.reshape(-1), q1, q2, kb, vb, lq1, lk1, lq2, lk2, sw, dq, dk_new, dv_new,
      *([cache_k] * pages), *([cache_v] * pages))


def _out_kernel(x_ref, o_ref, g_ref, w_ref, y_ref):
    gated = (o_ref[...] * _silu(g_ref[...])).astype(BF16)
    y_ref[...] = x_ref[...] + _dot(gated, w_ref[...])


def _out_proj(x2d, o, g, w, *, tt):
    n, d = x2d.shape
    tile = pl.BlockSpec((tt, d), lambda i: (i, 0))
    wide = pl.BlockSpec((tt, w.shape[0]), lambda i: (i, 0))
    return pl.pallas_call(
        _out_kernel,
        grid=(n // tt,),
        in_specs=[tile, wide, wide, _const_spec(w.shape)],
        out_specs=tile,
        out_shape=jax.ShapeDtypeStruct((n, d), F32),
        compiler_params=_params(("arbitrary",)),
        name="out_proj",
    )(x2d, o, g, w)


def kernel(x_prompt, x_sample, state_pool, state_conv, cache_k, cache_v, page_table, norm_w_ab, w_in_ab, pool_w, pool_scale, conv_w, conv_b, conv_ln_w, conv_ln_b, w_out_ab, norm_w_c, w_in_c, q_norm_w, k_norm_w, lambda_q1, lambda_k1, lambda_q2, lambda_k2, subln_w, w_out_c):
    nb, t, d = x_prompt.shape
    nd, tn, _ = x_sample.shape
    depth = norm_w_ab.shape[0] + norm_w_c.shape[0]
    row = lambda a: a.reshape(1, -1)

    yp, ys = x_prompt, x_sample
    pool_p, conv_p, k_p, v_p = [], [], [], []
    pool_s, conv_s, k_s, v_s = [], [], [], []
    for l in range(depth):
        j = l // 2
        if l % 2 == 0:
            weights = (row(norm_w_ab[j]), w_in_ab[j].astype(BF16), pool_w[j].astype(BF16), row(pool_scale[j]),
                       conv_w[j], row(conv_b[j]), row(conv_ln_w[j]), row(conv_ln_b[j]), w_out_ab[j].astype(BF16))
            yp, ptail, ctail = _ab_prompt(yp, *weights, tt=256)
            pool_p.append(ptail)
            conv_p.append(ctail)
            ys_t, xa_t, glu_t = _ab_sample(ys.transpose(1, 0, 2), state_pool[j].transpose(1, 0, 2),
                                           state_conv[j].transpose(1, 0, 2), *weights)
            ys = ys_t.transpose(1, 0, 2)
            pool_s.append(jnp.concatenate([state_pool[j], xa_t.transpose(1, 0, 2)], axis=1)[:, -POOL_BUF:])
            conv_s.append(jnp.concatenate([state_conv[j], glu_t.transpose(1, 0, 2)], axis=1)[:, -CONV_BUF:])
        else:
            lambda_init = 0.8 - 0.6 * math.exp(-0.3 * l)
            nw = row(norm_w_c[j])
            win = w_in_c[j].astype(BF16)
            wout = w_out_c[j].astype(BF16)
            qw2 = row(jnp.tile(q_norm_w[j], 2))
            kw2 = row(jnp.tile(k_norm_w[j], 2))
            lams = (row(lambda_q1[j]), row(lambda_k1[j]), row(lambda_q2[j]), row(lambda_k2[j]))
            sw = row(subln_w[j])
            cw = win.shape[1] // 4
            yp2 = yp.reshape(nb * t, d)
            ys2 = ys.reshape(nd * tn, d)
            tq = 512
            kp, vp, gp, q1, q2, kb, vb = _qkv(yp2, nw, win, qw2, kw2, tt=256, head_major=True, attn_block=tq)
            ks, vs, gs, qs = _qkv(ys2, nw, win, qw2, kw2, tt=nd * tn, head_major=False)
            rows_th = lambda a: a.reshape(nd, tn * N_HEADS, V_DIM)
            op, os_ = _attn(page_table, q1, q2, kb, vb, rows_th(qs), rows_th(ks), rows_th(vs), cache_k, cache_v,
                            *lams, sw, nb=nb, t=t, tq=tq, layer=j, pages=8, lambda_init=lambda_init)
            yp = _out_proj(yp2, op, gp, wout, tt=512).reshape(nb, t, d)
            ys = _out_proj(ys2, os_.reshape(nd * tn, cw), gs, wout, tt=nd * tn).reshape(nd, tn, d)
            k_p.append(kp.reshape(nb, t, N_HEADS, V_DIM))
            v_p.append(vp.reshape(nb, t, N_HEADS, V_DIM))
            k_s.append(ks.reshape(nd, tn, N_HEADS, V_DIM))
            v_s.append(vs.reshape(nd, tn, N_HEADS, V_DIM))
    return (yp, ys, jnp.stack(pool_p), jnp.stack(conv_p), jnp.stack(k_p), jnp.stack(v_p),
            jnp.stack(pool_s), jnp.stack(conv_s), jnp.stack(k_s), jnp.stack(v_s))
```

```python
import functools
import math

import jax
import jax.numpy as jnp
from jax import lax
from jax.experimental import pallas as pl
from jax.experimental.pallas import tpu as pltpu

F32 = jnp.float32
BF16 = jnp.bfloat16

POOL_WINDOWS = (2, 4, 8, 16)
POOL_BUF = max(POOL_WINDOWS) - 1
assert all(w & (w - 1) == 0 for w in POOL_WINDOWS)
CONV_K = 31
CONV_BUF = CONV_K - 1
N_HEADS = 8
HEAD_DIM = 64
V_DIM = 2 * HEAD_DIM
PAGE = 128
EPS = 1e-6
NEG_INF = -1e30
ATTN_SCALE = HEAD_DIM ** -0.5
LOG2E = math.log2(math.e)

SUBLANES = 8
LANES = 128
VMEM_LIMIT = 56 * 1024 * 1024

POOL_HALO = 16
CONV_HALO = 32
CONV_ROWS = 64
CONV_LANES = 128
KV_SPLIT = 2
LOOP_BLOCKS = 2


def _sigmoid(x):
    return 1.0 / (1.0 + jnp.exp(-x))


def _silu(x):
    return x * _sigmoid(x)


def _rms_rows(x, w):
    ms = jnp.mean(x * x, axis=-1, keepdims=True)
    return x * lax.rsqrt(ms + EPS) * w


def _dot(a, b):
    return jnp.dot(a, b, preferred_element_type=F32)


def _dot_nt(a, b):
    return lax.dot_general(a, b, (((1,), (1,)), ((), ())), preferred_element_type=F32)


def _const_spec(shape):
    n = len(shape)
    return pl.BlockSpec(shape, lambda *_: (0,) * n)


def _params(semantics):
    return pltpu.CompilerParams(dimension_semantics=semantics, vmem_limit_bytes=VMEM_LIMIT)


def _layernorm_silu(c, ln_w, ln_b):
    mu = jnp.mean(c, axis=-1, keepdims=True)
    xc = c - mu
    var = jnp.mean(xc * xc, axis=-1, keepdims=True)
    return _silu(xc * lax.rsqrt(var + EPS) * ln_w + ln_b)


def _ab_prompt_kernel(x_ref, nw_ref, win_ref, pw_ref, ps_ref, cw_ref, cb_ref, lnw_ref, lnb_ref, wout_ref,
                      y_ref, ptail_ref, ctail_ref, xbuf, gbuf, cbuf, *, tt, width):
    t = pl.program_id(1)
    nt = pl.num_programs(1)
    group = width // len(POOL_WINDOWS)

    @pl.when(t == 0)
    def _():
        xbuf[0:POOL_HALO, :] = jnp.zeros((POOL_HALO, width), F32)
        gbuf[0:CONV_HALO, :] = jnp.zeros((CONV_HALO, width), F32)

    x = x_ref[0]
    h = _rms_rows(x, nw_ref[...]).astype(BF16)

    u = _dot(h, win_ref[:, 2 * width:3 * width])
    v = _dot(h, win_ref[:, 3 * width:4 * width])
    gbuf[CONV_HALO:CONV_HALO + tt, :] = u * _sigmoid(v)

    xa = _dot(h, win_ref[:, 0:width])
    xbuf[POOL_HALO:POOL_HALO + tt, :] = xa
    pos = t * tt + lax.broadcasted_iota(jnp.int32, (tt, 1), 0)
    a_parts = []
    for g, w in enumerate(POOL_WINDOWS):
        sl = slice(g * group, (g + 1) * group)
        acc = xbuf[:, sl]
        j = 1
        while j < w:
            acc = acc + pltpu.roll(acc, j, 0)
            j *= 2
        acc = acc[POOL_HALO:, :]
        cnt = jnp.minimum(w, pos + 1).astype(F32)
        d = (acc / cnt - xa[:, sl]).astype(BF16)
        a_parts.append(_dot(d, pw_ref[g]))
    a = jnp.concatenate(a_parts, axis=-1) * ps_ref[...]
    ga = _dot(h, win_ref[:, width:2 * width])
    mix_a = (a * _silu(ga)).astype(BF16)
    gb = _dot(h, win_ref[:, 4 * width:5 * width])
    y_a = x + _dot(mix_a, wout_ref[0:width, :])

    first_row = CONV_HALO - CONV_BUF
    rows = CONV_ROWS + CONV_HALO
    for r0 in range(0, tt, CONV_ROWS):
        for lc in range(width // CONV_LANES):
            ls = slice(lc * CONV_LANES, (lc + 1) * CONV_LANES)
            win = gbuf[r0:r0 + rows, ls]
            acc = jnp.zeros((CONV_ROWS, CONV_LANES), F32)
            for r in range(SUBLANES):
                taps = [k for k in range(CONV_K) if (first_row + k) % SUBLANES == r]
                if not taps:
                    continue
                shifted = pltpu.roll(win, rows - r, 0) if r else win
                for k in taps:
                    o = (first_row + k) // SUBLANES * SUBLANES
                    acc = acc + shifted[o:o + CONV_ROWS, :] * cw_ref[k:k + 1, ls]
            cbuf[r0:r0 + CONV_ROWS, ls] = acc
    c = _layernorm_silu(cbuf[...] + cb_ref[...], lnw_ref[...], lnb_ref[...])
    mix_b = (c * _silu(gb)).astype(BF16)

    y_ref[0] = y_a + _dot(mix_b, wout_ref[width:2 * width, :])

    @pl.when(t == nt - 1)
    def _():
        ptail_ref[0] = xbuf[POOL_HALO + tt - POOL_BUF:POOL_HALO + tt, :]
        ctail_ref[0] = gbuf[CONV_HALO + tt - CONV_BUF:CONV_HALO + tt, :]

    xbuf[0:POOL_HALO, :] = xbuf[tt:tt + POOL_HALO, :]
    gbuf[0:CONV_HALO, :] = gbuf[tt:tt + CONV_HALO, :]


def _ab_prompt(x, nw, win, pw, ps, cw, cb, lnw, lnb, wout, *, tt):
    b, t, d = x.shape
    width = ps.shape[-1]
    kern = functools.partial(_ab_prompt_kernel, tt=tt, width=width)
    tile = pl.BlockSpec((1, tt, d), lambda i, j: (i, j, 0))
    return pl.pallas_call(
        kern,
        grid=(b, t // tt),
        in_specs=[tile, _const_spec(nw.shape), _const_spec(win.shape), _const_spec(pw.shape),
                  _const_spec(ps.shape), _const_spec(cw.shape), _const_spec(cb.shape),
                  _const_spec(lnw.shape), _const_spec(lnb.shape), _const_spec(wout.shape)],
        out_specs=[tile,
                   pl.BlockSpec((1, POOL_BUF, width), lambda i, j: (i, 0, 0)),
                   pl.BlockSpec((1, CONV_BUF, width), lambda i, j: (i, 0, 0))],
        out_shape=[jax.ShapeDtypeStruct((b, t, d), F32),
                   jax.ShapeDtypeStruct((b, POOL_BUF, width), F32),
                   jax.ShapeDtypeStruct((b, CONV_BUF, width), F32)],
        scratch_shapes=[pltpu.VMEM((POOL_HALO + tt, width), F32),
                        pltpu.VMEM((CONV_HALO + tt, width), F32),
                        pltpu.VMEM((tt, width), F32)],
        compiler_params=_params(("arbitrary", "arbitrary")),
        name="ab_prompt",
    )(x, nw, win, pw, ps, cw, cb, lnw, lnb, wout)


def _ab_sample_kernel(x_ref, sp_ref, sc_ref, nw_ref, win_ref, pw_ref, ps_ref, cw_ref, cb_ref, lnw_ref, lnb_ref,
                      wout_ref, y_ref, xa_ref, glu_ref, *, tn, nb, width):
    group = width // len(POOL_WINDOWS)
    x = x_ref[...].reshape(tn * nb, x_ref.shape[-1])
    h = _rms_rows(x, nw_ref[...]).astype(BF16)

    xa = _dot(h, win_ref[:, 0:width])
    xa_ref[...] = xa.reshape(tn, nb, width)

    def pool_row(j):
        return sp_ref[j] if j < POOL_BUF else xa[(j - POOL_BUF) * nb:(j - POOL_BUF + 1) * nb, :]

    d_rows = []
    for ti in range(tn):
        parts = []
        for g, w in enumerate(POOL_WINDOWS):
            sl = slice(g * group, (g + 1) * group)
            acc = pool_row(POOL_BUF + ti)[:, sl]
            for j in range(1, w):
                acc = acc + pool_row(POOL_BUF + ti - j)[:, sl]
            cnt = float(min(w, ti + 1 + POOL_BUF))
            parts.append(acc / cnt - pool_row(POOL_BUF + ti)[:, sl])
        d_rows.append(jnp.concatenate(parts, axis=-1))
    d = jnp.concatenate(d_rows, axis=0).astype(BF16)
    a = jnp.concatenate([_dot(d[:, g * group:(g + 1) * group], pw_ref[g]) for g in range(len(POOL_WINDOWS))],
                        axis=-1) * ps_ref[...]
    ga = _dot(h, win_ref[:, width:2 * width])
    mix_a = (a * _silu(ga)).astype(BF16)

    u = _dot(h, win_ref[:, 2 * width:3 * width])
    v = _dot(h, win_ref[:, 3 * width:4 * width])
    glu = u * _sigmoid(v)
    glu_ref[...] = glu.reshape(tn, nb, width)

    def conv_row(j):
        return sc_ref[j] if j < CONV_BUF else glu[(j - CONV_BUF) * nb:(j - CONV_BUF + 1) * nb, :]

    c_rows = []
    for ti in range(tn):
        acc = jnp.zeros((nb, width), F32)
        for k in range(CONV_K):
            acc = acc + conv_row(ti + k) * cw_ref[k:k + 1, :]
        c_rows.append(acc)
    c = _layernorm_silu(jnp.concatenate(c_rows, axis=0) + cb_ref[...], lnw_ref[...], lnb_ref[...])
    gb = _dot(h, win_ref[:, 4 * width:5 * width])
    mix_b = (c * _silu(gb)).astype(BF16)

    y = x + _dot(mix_a, wout_ref[0:width, :]) + _dot(mix_b, wout_ref[width:2 * width, :])
    y_ref[...] = y.reshape(tn, nb, y_ref.shape[-1])


def _ab_sample(x_t, sp_t, sc_t, nw, win, pw, ps, cw, cb, lnw, lnb, wout):
    tn, nb, d = x_t.shape
    width = ps.shape[-1]
    kern = functools.partial(_ab_sample_kernel, tn=tn, nb=nb, width=width)
    args = (x_t, sp_t, sc_t, nw, win, pw, ps, cw, cb, lnw, lnb, wout)
    return pl.pallas_call(
        kern,
        grid=(1,),
        in_specs=[_const_spec(a.shape) for a in args],
        out_specs=[_const_spec((tn, nb, d)), _const_spec((tn, nb, width)), _const_spec((tn, nb, width))],
        out_shape=[jax.ShapeDtypeStruct((tn, nb, d), F32),
                   jax.ShapeDtypeStruct((tn, nb, width), F32),
                   jax.ShapeDtypeStruct((tn, nb, width), F32)],
        compiler_params=_params(("arbitrary",)),
        name="ab_sample",
    )(*args)


def _subhead_norm(x, w2):
    lane = lax.broadcasted_iota(jnp.int32, x.shape, 1)
    lo = lane < HEAD_DIM
    sq = x * x
    s_lo = jnp.sum(jnp.where(lo, sq, 0.0), axis=-1, keepdims=True)
    s_hi = jnp.sum(jnp.where(lo, 0.0, sq), axis=-1, keepdims=True)
    ms = jnp.where(lo, s_lo, s_hi) * (1.0 / HEAD_DIM)
    return x * lax.rsqrt(ms + EPS) * w2


def _qkv_kernel(x_ref, nw_ref, win_ref, qw_ref, kw_ref, *out_refs, cw, head_major):
    x = x_ref[...]
    h = _rms_rows(x, nw_ref[...]).astype(BF16)
    q = _dot(h, win_ref[:, 0:cw])
    k = _dot(h, win_ref[:, cw:2 * cw])
    v = _dot(h, win_ref[:, 2 * cw:3 * cw])
    g = _dot(h, win_ref[:, 3 * cw:4 * cw])
    if head_major:
        k_ref, v_ref, g_ref, q1_ref, q2_ref, kb_ref, vb_ref = out_refs
    else:
        k_ref, v_ref, g_ref, q_ref = out_refs
    v_ref[...] = v
    g_ref[...] = g.astype(g_ref.dtype)
    for hd in range(N_HEADS):
        sl = slice(hd * V_DIM, (hd + 1) * V_DIM)
        qn = _subhead_norm(q[:, sl], qw_ref[...]) * (ATTN_SCALE * LOG2E)
        kn = _subhead_norm(k[:, sl], kw_ref[...])
        k_ref[:, sl] = kn
        if head_major:
            qt = qn.T
            lo = lax.broadcasted_iota(jnp.int32, qt.shape, 0) < HEAD_DIM
            q1_ref[hd, 0] = jnp.where(lo, qt, 0.0).astype(BF16)
            q2_ref[hd, 0] = jnp.where(lo, 0.0, qt).astype(BF16)
            kb_ref[hd] = kn.astype(BF16)
            vb_ref[hd, 0] = v[:, sl].T.astype(BF16)
        else:
            q_ref[:, sl] = qn


def _qkv(x2d, nw, win, qw2, kw2, *, tt, head_major, attn_block=None):
    n, d = x2d.shape
    cw = win.shape[1] // 4
    tile = pl.BlockSpec((tt, d), lambda i: (i, 0))
    wide = pl.BlockSpec((tt, cw), lambda i: (i, 0))
    out_specs = [wide, wide, wide]
    out_shape = [jax.ShapeDtypeStruct((n, cw), F32)] * 2 + [jax.ShapeDtypeStruct((n, cw), BF16)]
    if head_major:
        per = attn_block // tt
        hm = pl.BlockSpec((N_HEADS, tt, V_DIM), lambda i: (0, i, 0))
        hmt = pl.BlockSpec((N_HEADS, 1, V_DIM, tt), lambda i: (0, i // per, 0, i % per))
        out_specs += [hmt, hmt, hm, hmt]
        tshape = jax.ShapeDtypeStruct((N_HEADS, n // attn_block, V_DIM, attn_block), BF16)
        out_shape += [tshape, tshape, jax.ShapeDtypeStruct((N_HEADS, n, V_DIM), BF16), tshape]
    else:
        out_specs += [wide]
        out_shape += [jax.ShapeDtypeStruct((n, cw), F32)]
    return pl.pallas_call(
        functools.partial(_qkv_kernel, cw=cw, head_major=head_major),
        grid=(n // tt,),
        in_specs=[tile, _const_spec(nw.shape), _const_spec(win.shape), _const_spec(qw2.shape),
                  _const_spec(kw2.shape)],
        out_specs=out_specs,
        out_shape=out_shape,
        compiler_params=_params(("arbitrary",)),
        name="qkv_prompt" if head_major else "qkv_sample",
    )(x2d, nw, win, qw2, kw2)


def _lambda(lq1_ref, lk1_ref, lq2_ref, lk2_ref, lambda_init):
    a = jnp.sum(lq1_ref[...] * lk1_ref[...], axis=-1, keepdims=True)
    b = jnp.sum(lq2_ref[...] * lk2_ref[...], axis=-1, keepdims=True)
    return jnp.exp(a) - jnp.exp(b) + lambda_init


def _fold_sublanes(x, op):
    shift = SUBLANES // 2
    while shift:
        x = op(x, pltpu.roll(x, shift, 0))
        shift //= 2
    return x


def _col_reduce(x3, op, reduce_leading):
    return _fold_sublanes(reduce_leading(x3, axis=0), op)


def _prompt_attn_step(qi, q1_ref, q2_ref, k_ref, v_ref, lam, sw_ref, o_ref, m1, l1, a1, m2, l2, a2, *,
                      tq, lambda_init, co_scores, co_finish):
    q1 = q1_ref[0, 0]
    q2 = q2_ref[0, 0]
    tkp = tq // KV_SPLIT
    nd = V_DIM // SUBLANES

    for m, l, a in ((m1, l1, a1), (m2, l2, a2)):
        m[...] = jnp.full(m.shape, NEG_INF, F32)
        l[...] = jnp.zeros(l.shape, F32)
        a[...] = jnp.zeros(a.shape, F32)

    def scores(j, diagonal=False):
        chains = []
        ones = jnp.ones((2 * SUBLANES, tkp), BF16)
        for part in range(KV_SPLIT):
            lo = part * tkp
            q_lo = lo if diagonal else 0
            k = k_ref[0, pl.ds(pl.multiple_of(j * tq, tq) + lo, tkp), :]
            vt = jnp.concatenate([v_ref[0, j, :, lo:lo + tkp], ones], axis=0)
            for q, state in ((q1, (m1, l1, a1)), (q2, (m2, l2, a2))):
                chains.append((_dot(k, q[:, q_lo:]), vt, lo, q_lo, state))
        return chains

    def softmax_pv(chains, masked):
        for s, vt, lo, q_lo, (m, l, a) in chains:
            nq = tq - q_lo
            if masked:
                key = lo + lax.broadcasted_iota(jnp.int32, s.shape, 0)
                qry = q_lo + lax.broadcasted_iota(jnp.int32, s.shape, 1)
                s = jnp.where(key <= qry, s, NEG_INF)
            s3 = s.reshape(tkp // SUBLANES, SUBLANES, nq)
            m_old = m[:, q_lo:]
            m_new = jnp.maximum(m_old, _col_reduce(s3, jnp.maximum, jnp.max))
            p = jnp.exp2(s3 - m_new[None]).reshape(tkp, nq).astype(BF16)
            alpha = jnp.exp2(m_old - m_new)
            pv = _dot(vt, p)
            l[:, q_lo:] = alpha * l[:, q_lo:] + pv[V_DIM:V_DIM + SUBLANES, :]
            a[:, q_lo:] = ((alpha[None] * a[:, q_lo:].reshape(nd, SUBLANES, nq)).reshape(V_DIM, nq)
                           + pv[0:V_DIM, :])
            m[:, q_lo:] = m_new

    def body(jj, carry):
        base = jj * LOOP_BLOCKS
        chains = []
        for u in range(LOOP_BLOCKS):
            chains += scores(base + u)
        softmax_pv(chains, False)
        return carry

    n_loop = jnp.maximum(qi - 1, 0)
    n_multi = n_loop // LOOP_BLOCKS
    lax.fori_loop(0, n_multi, body, 0)
    for u in range(LOOP_BLOCKS - 1):
        @pl.when(n_multi * LOOP_BLOCKS + u < n_loop)
        def _():
            softmax_pv(scores(n_multi * LOOP_BLOCKS + u), False)

    def final(with_previous):
        plain = scores(qi - 1) if with_previous else []
        masked = scores(qi, diagonal=True)
        co = co_scores()
        softmax_pv(plain, False)
        softmax_pv(masked, True)
        co_finish(co)

    @pl.when(qi > 0)
    def _():
        final(True)

    @pl.when(qi == 0)
    def _():
        final(False)

    o3 = (a1[...].reshape(nd, SUBLANES, tq) * (1.0 / l1[...])[None]
          - lam * (a2[...].reshape(nd, SUBLANES, tq) * (1.0 / l2[...])[None]))
    ms = _col_reduce(o3 * o3, jnp.add, jnp.sum) * (1.0 / V_DIM)
    on = (o3 * lax.rsqrt(ms + EPS)[None]).reshape(V_DIM, tq)
    o_ref[...] = (on.T * sw_ref[...] * (1.0 - lambda_init)).astype(o_ref.dtype)


def _sample_attn_step(step, last, q_ref, kn_ref, vn_ref, k_refs, v_refs, lam, sw_ref, o_ref,
                      wq, newk, newv, m_ref, l_ref, acc, *, tn, lambda_init):
    half = tn * N_HEADS
    page_rows = PAGE * N_HEADS

    def same_head(shape):
        row = lax.broadcasted_iota(jnp.int32, shape, 0)
        col = lax.broadcasted_iota(jnp.int32, shape, 1)
        return (col & (N_HEADS - 1)) == (row & (N_HEADS - 1))

    def update(s_list, v_list):
        m_old = m_ref[...]
        m_new = m_old
        for s in s_list:
            m_new = jnp.maximum(m_new, jnp.max(s, axis=-1, keepdims=True))
        alpha = jnp.exp2(m_old - m_new)
        l_new = alpha * l_ref[...]
        a_new = alpha * acc[...]
        for s, v in zip(s_list, v_list):
            p = jnp.exp2(s - m_new)
            l_new = l_new + jnp.sum(p, axis=-1, keepdims=True)
            a_new = a_new + _dot(p.astype(BF16), v)
        m_ref[...] = m_new
        l_ref[...] = l_new
        acc[...] = a_new

    def start():
        @pl.when(step == 0)
        def _():
            qb = q_ref[0]
            lo = lax.broadcasted_iota(jnp.int32, qb.shape, 1) < HEAD_DIM
            wq[...] = jnp.concatenate([jnp.where(lo, qb, 0.0), jnp.where(lo, 0.0, qb)], axis=0).astype(BF16)
            newk[...] = jnp.zeros(newk.shape, F32)
            newv[...] = jnp.zeros(newv.shape, F32)
            newk[0:half, :] = kn_ref[0]
            newv[0:half, :] = vn_ref[0]
            m_ref[...] = jnp.full(m_ref.shape, NEG_INF, F32)
            l_ref[...] = jnp.zeros(l_ref.shape, F32)
            acc[...] = jnp.zeros(acc.shape, F32)
            s = _dot_nt(wq[...], newk[...].astype(BF16))
            row = lax.broadcasted_iota(jnp.int32, s.shape, 0)
            col = lax.broadcasted_iota(jnp.int32, s.shape, 1)
            causal = (col >> 3) <= ((row >> 3) & (tn - 1))
            s = jnp.where(same_head(s.shape), jnp.where(causal, s, NEG_INF), NEG_INF)
            update([s], [newv[...].astype(BF16)])

    def scores():
        w = wq[...]
        return [_dot_nt(w, kr[...].reshape(page_rows, V_DIM).astype(BF16)) for kr in k_refs]

    def finish(raw):
        mask = same_head((2 * half, page_rows))
        update([jnp.where(mask, s, NEG_INF) for s in raw],
               [vr[...].reshape(page_rows, V_DIM).astype(BF16) for vr in v_refs])

        @pl.when(step == last)
        def _():
            o = acc[...] / l_ref[...]
            od = o[0:half, :] - lam * o[half:2 * half, :]
            o_ref[0] = (_rms_rows(od, sw_ref[...]) * (1.0 - lambda_init)).astype(o_ref.dtype)

    return start, scores, finish


def _attn_kernel(pt_ref, q1_ref, q2_ref, k_ref, v_ref, lq1_ref, lk1_ref, lq2_ref, lk2_ref, sw_ref,
                 dq_ref, dkn_ref, dvn_ref, *refs, pages, chunks, tq, tn, lambda_init):
    k_refs = refs[:pages]
    v_refs = refs[pages:2 * pages]
    o_ref, do_ref = refs[2 * pages:2 * pages + 2]
    prompt_state = refs[2 * pages + 2:2 * pages + 8]
    sample_state = refs[2 * pages + 8:]
    lam = _lambda(lq1_ref, lk1_ref, lq2_ref, lk2_ref, lambda_init)
    qi = pl.program_id(2)
    lin = (pl.program_id(0) * pl.num_programs(1) + pl.program_id(1)) * pl.num_programs(2) + qi
    start, scores, finish = _sample_attn_step(lin % chunks, chunks - 1, dq_ref, dkn_ref, dvn_ref, k_refs, v_refs,
                                              lam, sw_ref, do_ref, *sample_state, tn=tn, lambda_init=lambda_init)
    start()
    _prompt_attn_step(qi, q1_ref, q2_ref, k_ref, v_ref, lam, sw_ref, o_ref, *prompt_state,
                      tq=tq, lambda_init=lambda_init, co_scores=scores, co_finish=finish)


def _attn(page_table, q1, q2, kb, vb, dq, dk_new, dv_new, cache_k, cache_v, lq1, lk1, lq2, lk2, sw, *,
          nb, t, tq, layer, pages, lambda_init):
    nq = t // tq
    nd, half, _ = dq.shape
    tn = half // N_HEADS
    n_pages = page_table.shape[1]
    chunks = n_pages // pages
    assert nb * N_HEADS * nq == nd * chunks, "one chunk of sample pages per prompt grid step"

    def lin(b, h, i):
        return (b * N_HEADS + h) * nq + i

    qspec = pl.BlockSpec((1, 1, V_DIM, tq), lambda b, h, i, pt: (h, b * nq + i, 0, 0))
    kspec = pl.BlockSpec((1, t, V_DIM), lambda b, h, i, pt: (h, b, 0))
    vspec = pl.BlockSpec((1, nq, V_DIM, tq), lambda b, h, i, pt: (h, b, 0, 0))
    vec = pl.BlockSpec(lq1.shape, lambda b, h, i, pt: (0, 0))
    tok = pl.BlockSpec((1, half, V_DIM), lambda b, h, i, pt: (lin(b, h, i) // chunks, 0, 0))

    def page_spec(p):
        def index(b, h, i, pt):
            step = lin(b, h, i)
            return (layer, pt[(step // chunks) * n_pages + (step % chunks) * pages + p], 0, 0, 0)
        return pl.BlockSpec((None, None, PAGE, N_HEADS, V_DIM), index)

    rows = 2 * half
    grid_spec = pltpu.PrefetchScalarGridSpec(
        num_scalar_prefetch=1,
        grid=(nb, N_HEADS, nq),
        in_specs=[qspec, qspec, kspec, vspec, vec, vec, vec, vec,
                  pl.BlockSpec(sw.shape, lambda b, h, i, pt: (0, 0)), tok, tok, tok]
                 + [page_spec(p) for p in range(pages)] * 2,
        out_specs=[pl.BlockSpec((tq, V_DIM), lambda b, h, i, pt: (b * nq + i, h)), tok],
        scratch_shapes=[pltpu.VMEM((SUBLANES, tq), F32), pltpu.VMEM((SUBLANES, tq), F32),
                        pltpu.VMEM((V_DIM, tq), F32)] * 2
                       + [pltpu.VMEM((rows, V_DIM), BF16),
                          pltpu.VMEM((LANES, V_DIM), F32), pltpu.VMEM((LANES, V_DIM), F32),
                          pltpu.VMEM((rows, 1), F32), pltpu.VMEM((rows, 1), F32),
                          pltpu.VMEM((rows, V_DIM), F32)],
    )
    return pl.pallas_call(
        functools.partial(_attn_kernel, pages=pages, chunks=chunks, tq=tq, tn=tn, lambda_init=lambda_init),
        grid_spec=grid_spec,
        out_shape=[jax.ShapeDtypeStruct((nb * t, N_HEADS * V_DIM), BF16),
                   jax.ShapeDtypeStruct((nd, half, V_DIM), BF16)],
        compiler_params=_params(("arbitrary", "arbitrary", "arbitrary")),
        name="attn",
    )(page_table.reshape(-1), q1, q2, kb, vb, lq1, lk1, lq2, lk2, sw, dq, dk_new, dv_new,
      *([cache_k] * pages), *([cache_v] * pages))


def _out_kernel(x_ref, o_ref, g_ref, w_ref, y_ref):
    gated = (o_ref[...].astype(F32) * _silu(g_ref[...].astype(F32))).astype(BF16)
    y_ref[...] = x_ref[...] + _dot(gated, w_ref[...])


def _out_proj(x2d, o, g, w, *, tt):
    n, d = x2d.shape
    tile = pl.BlockSpec((tt, d), lambda i: (i, 0))
    wide = pl.BlockSpec((tt, w.shape[0]), lambda i: (i, 0))
    return pl.pallas_call(
        _out_kernel,
        grid=(n // tt,),
        in_specs=[tile, wide, wide, _const_spec(w.shape)],
        out_specs=tile,
        out_shape=jax.ShapeDtypeStruct((n, d), F32),
        compiler_params=_params(("arbitrary",)),
        name="out_proj",
    )(x2d, o, g, w)


def kernel(x_prompt, x_sample, state_pool, state_conv, cache_k, cache_v, page_table, norm_w_ab, w_in_ab, pool_w, pool_scale, conv_w, conv_b, conv_ln_w, conv_ln_b, w_out_ab, norm_w_c, w_in_c, q_norm_w, k_norm_w, lambda_q1, lambda_k1, lambda_q2, lambda_k2, subln_w, w_out_c):
    nb, t, d = x_prompt.shape
    nd, tn, _ = x_sample.shape
    depth = norm_w_ab.shape[0] + norm_w_c.shape[0]
    row = lambda a: a.reshape(1, -1)

    yp, ys = x_prompt, x_sample
    pool_p, conv_p, k_p, v_p = [], [], [], []
    pool_s, conv_s, k_s, v_s = [], [], [], []
    for l in range(depth):
        j = l // 2
        if l % 2 == 0:
            weights = (row(norm_w_ab[j]), w_in_ab[j].astype(BF16), pool_w[j].astype(BF16), row(pool_scale[j]),
                       conv_w[j], row(conv_b[j]), row(conv_ln_w[j]), row(conv_ln_b[j]), w_out_ab[j].astype(BF16))
            yp, ptail, ctail = _ab_prompt(yp, *weights, tt=256)
            pool_p.append(ptail)
            conv_p.append(ctail)
            ys_t, xa_t, glu_t = _ab_sample(ys.transpose(1, 0, 2), state_pool[j].transpose(1, 0, 2),
                                           state_conv[j].transpose(1, 0, 2), *weights)
            ys = ys_t.transpose(1, 0, 2)
            pool_s.append(jnp.concatenate([state_pool[j], xa_t.transpose(1, 0, 2)], axis=1)[:, -POOL_BUF:])
            conv_s.append(jnp.concatenate([state_conv[j], glu_t.transpose(1, 0, 2)], axis=1)[:, -CONV_BUF:])
        else:
            lambda_init = 0.8 - 0.6 * math.exp(-0.3 * l)
            nw = row(norm_w_c[j])
            win = w_in_c[j].astype(BF16)
            wout = w_out_c[j].astype(BF16)
            qw2 = row(jnp.tile(q_norm_w[j], 2))
            kw2 = row(jnp.tile(k_norm_w[j], 2))
            lams = (row(lambda_q1[j]), row(lambda_k1[j]), row(lambda_q2[j]), row(lambda_k2[j]))
            sw = row(subln_w[j])
            cw = win.shape[1] // 4
            yp2 = yp.reshape(nb * t, d)
            ys2 = ys.reshape(nd * tn, d)
            tq = 512
            kp, vp, gp, q1, q2, kb, vb = _qkv(yp2, nw, win, qw2, kw2, tt=256, head_major=True, attn_block=tq)
            ks, vs, gs, qs = _qkv(ys2, nw, win, qw2, kw2, tt=nd * tn, head_major=False)
            rows_th = lambda a: a.reshape(nd, tn * N_HEADS, V_DIM)
            op, os_ = _attn(page_table, q1, q2, kb, vb, rows_th(qs), rows_th(ks), rows_th(vs), cache_k, cache_v,
                            *lams, sw, nb=nb, t=t, tq=tq, layer=j, pages=8, lambda_init=lambda_init)
            yp = _out_proj(yp2, op, gp, wout, tt=512).reshape(nb, t, d)
            ys = _out_proj(ys2, os_.reshape(nd * tn, cw), gs, wout, tt=nd * tn).reshape(nd, tn, d)
            k_p.append(kp.reshape(nb, t, N_HEADS, V_DIM))
            v_p.append(vp.reshape(nb, t, N_HEADS, V_DIM))
            k_s.append(ks.reshape(nd, tn, N_HEADS, V_DIM))
            v_s.append(vs.reshape(nd, tn, N_HEADS, V_DIM))
    return (yp, ys, jnp.stack(pool_p), jnp.stack(conv_p), jnp.stack(k_p), jnp.stack(v_p),
            jnp.stack(pool_s), jnp.stack(conv_s), jnp.stack(k_s), jnp.stack(v_s))
```

```python
import functools
import math

import jax
import jax.numpy as jnp
from jax import lax
from jax.experimental import pallas as pl
from jax.experimental.pallas import tpu as pltpu

F32 = jnp.float32
BF16 = jnp.bfloat16

POOL_WINDOWS = (2, 4, 8, 16)
POOL_BUF = max(POOL_WINDOWS) - 1
assert all(w & (w - 1) == 0 for w in POOL_WINDOWS)
CONV_K = 31
CONV_BUF = CONV_K - 1
N_HEADS = 8
HEAD_DIM = 64
V_DIM = 2 * HEAD_DIM
PAGE = 128
EPS = 1e-6
NEG_INF = -1e30
ATTN_SCALE = HEAD_DIM ** -0.5
LOG2E = math.log2(math.e)

SUBLANES = 8
LANES = 128
VMEM_LIMIT = 56 * 1024 * 1024

POOL_HALO = 16
CONV_HALO = 32
CONV_ROWS = 64
CONV_LANES = 128
KV_SPLIT = 2
LOOP_BLOCKS = 2
FINAL_BLOCKS = 3


def _sigmoid(x):
    return 1.0 / (1.0 + jnp.exp(-x))


def _silu(x):
    return x * _sigmoid(x)


def _rms_rows(x, w):
    ms = jnp.mean(x * x, axis=-1, keepdims=True)
    return x * lax.rsqrt(ms + EPS) * w


def _dot(a, b):
    return jnp.dot(a, b, preferred_element_type=F32)


def _dot_nt(a, b):
    return lax.dot_general(a, b, (((1,), (1,)), ((), ())), preferred_element_type=F32)


def _const_spec(shape):
    n = len(shape)
    return pl.BlockSpec(shape, lambda *_: (0,) * n)


def _params(semantics):
    return pltpu.CompilerParams(dimension_semantics=semantics, vmem_limit_bytes=VMEM_LIMIT)


def _layernorm_silu(c, ln_w, ln_b):
    mu = jnp.mean(c, axis=-1, keepdims=True)
    xc = c - mu
    var = jnp.mean(xc * xc, axis=-1, keepdims=True)
    return _silu(xc * lax.rsqrt(var + EPS) * ln_w + ln_b)


def _ab_prompt_kernel(x_ref, nw_ref, win_ref, pw_ref, ps_ref, cw_ref, cb_ref, lnw_ref, lnb_ref, wout_ref,
                      y_ref, ptail_ref, ctail_ref, xbuf, gbuf, cbuf, *, tt, width):
    t = pl.program_id(1)
    nt = pl.num_programs(1)
    group = width // len(POOL_WINDOWS)

    @pl.when(t == 0)
    def _():
        xbuf[0:POOL_HALO, :] = jnp.zeros((POOL_HALO, width), F32)
        gbuf[0:CONV_HALO, :] = jnp.zeros((CONV_HALO, width), F32)

    x = x_ref[0]
    h = _rms_rows(x, nw_ref[...]).astype(BF16)

    u = _dot(h, win_ref[:, 2 * width:3 * width])
    v = _dot(h, win_ref[:, 3 * width:4 * width])
    gbuf[CONV_HALO:CONV_HALO + tt, :] = u * _sigmoid(v)

    xa = _dot(h, win_ref[:, 0:width])
    xbuf[POOL_HALO:POOL_HALO + tt, :] = xa
    pos = t * tt + lax.broadcasted_iota(jnp.int32, (tt, 1), 0)
    a_parts = []
    for g, w in enumerate(POOL_WINDOWS):
        sl = slice(g * group, (g + 1) * group)
        acc = xbuf[:, sl]
        j = 1
        while j < w:
            acc = acc + pltpu.roll(acc, j, 0)
            j *= 2
        acc = acc[POOL_HALO:, :]
        cnt = jnp.minimum(w, pos + 1).astype(F32)
        d = (acc / cnt - xa[:, sl]).astype(BF16)
        a_parts.append(_dot(d, pw_ref[g]))
    a = jnp.concatenate(a_parts, axis=-1) * ps_ref[...]
    ga = _dot(h, win_ref[:, width:2 * width])
    mix_a = (a * _silu(ga)).astype(BF16)
    gb = _dot(h, win_ref[:, 4 * width:5 * width])
    y_a = x + _dot(mix_a, wout_ref[0:width, :])

    first_row = CONV_HALO - CONV_BUF
    rows = CONV_ROWS + CONV_HALO
    for r0 in range(0, tt, CONV_ROWS):
        for lc in range(width // CONV_LANES):
            ls = slice(lc * CONV_LANES, (lc + 1) * CONV_LANES)
            win = gbuf[r0:r0 + rows, ls]
            acc = jnp.zeros((CONV_ROWS, CONV_LANES), F32)
            for r in range(SUBLANES):
                taps = [k for k in range(CONV_K) if (first_row + k) % SUBLANES == r]
                if not taps:
                    continue
                shifted = pltpu.roll(win, rows - r, 0) if r else win
                for k in taps:
                    o = (first_row + k) // SUBLANES * SUBLANES
                    acc = acc + shifted[o:o + CONV_ROWS, :] * cw_ref[k:k + 1, ls]
            cbuf[r0:r0 + CONV_ROWS, ls] = acc
    c = _layernorm_silu(cbuf[...] + cb_ref[...], lnw_ref[...], lnb_ref[...])
    mix_b = (c * _silu(gb)).astype(BF16)

    y_ref[0] = y_a + _dot(mix_b, wout_ref[width:2 * width, :])

    @pl.when(t == nt - 1)
    def _():
        ptail_ref[0] = xbuf[POOL_HALO + tt - POOL_BUF:POOL_HALO + tt, :]
        ctail_ref[0] = gbuf[CONV_HALO + tt - CONV_BUF:CONV_HALO + tt, :]

    xbuf[0:POOL_HALO, :] = xbuf[tt:tt + POOL_HALO, :]
    gbuf[0:CONV_HALO, :] = gbuf[tt:tt + CONV_HALO, :]


def _ab_prompt(x, nw, win, pw, ps, cw, cb, lnw, lnb, wout, *, tt):
    b, t, d = x.shape
    width = ps.shape[-1]
    kern = functools.partial(_ab_prompt_kernel, tt=tt, width=width)
    tile = pl.BlockSpec((1, tt, d), lambda i, j: (i, j, 0))
    return pl.pallas_call(
        kern,
        grid=(b, t // tt),
        in_specs=[tile, _const_spec(nw.shape), _const_spec(win.shape), _const_spec(pw.shape),
                  _const_spec(ps.shape), _const_spec(cw.shape), _const_spec(cb.shape),
                  _const_spec(lnw.shape), _const_spec(lnb.shape), _const_spec(wout.shape)],
        out_specs=[tile,
                   pl.BlockSpec((1, POOL_BUF, width), lambda i, j: (i, 0, 0)),
                   pl.BlockSpec((1, CONV_BUF, width), lambda i, j: (i, 0, 0))],
        out_shape=[jax.ShapeDtypeStruct((b, t, d), F32),
                   jax.ShapeDtypeStruct((b, POOL_BUF, width), F32),
                   jax.ShapeDtypeStruct((b, CONV_BUF, width), F32)],
        scratch_shapes=[pltpu.VMEM((POOL_HALO + tt, width), F32),
                        pltpu.VMEM((CONV_HALO + tt, width), F32),
                        pltpu.VMEM((tt, width), F32)],
        compiler_params=_params(("arbitrary", "arbitrary")),
        name="ab_prompt",
    )(x, nw, win, pw, ps, cw, cb, lnw, lnb, wout)


def _ab_sample_kernel(x_ref, sp_ref, sc_ref, nw_ref, win_ref, pw_ref, ps_ref, cw_ref, cb_ref, lnw_ref, lnb_ref,
                      wout_ref, y_ref, xa_ref, glu_ref, *, tn, nb, width):
    group = width // len(POOL_WINDOWS)
    x = x_ref[...].reshape(tn * nb, x_ref.shape[-1])
    h = _rms_rows(x, nw_ref[...]).astype(BF16)

    xa = _dot(h, win_ref[:, 0:width])
    xa_ref[...] = xa.reshape(tn, nb, width)

    def pool_row(j):
        return sp_ref[j] if j < POOL_BUF else xa[(j - POOL_BUF) * nb:(j - POOL_BUF + 1) * nb, :]

    d_rows = []
    for ti in range(tn):
        parts = []
        for g, w in enumerate(POOL_WINDOWS):
            sl = slice(g * group, (g + 1) * group)
            acc = pool_row(POOL_BUF + ti)[:, sl]
            for j in range(1, w):
                acc = acc + pool_row(POOL_BUF + ti - j)[:, sl]
            cnt = float(min(w, ti + 1 + POOL_BUF))
            parts.append(acc / cnt - pool_row(POOL_BUF + ti)[:, sl])
        d_rows.append(jnp.concatenate(parts, axis=-1))
    d = jnp.concatenate(d_rows, axis=0).astype(BF16)
    a = jnp.concatenate([_dot(d[:, g * group:(g + 1) * group], pw_ref[g]) for g in range(len(POOL_WINDOWS))],
                        axis=-1) * ps_ref[...]
    ga = _dot(h, win_ref[:, width:2 * width])
    mix_a = (a * _silu(ga)).astype(BF16)

    u = _dot(h, win_ref[:, 2 * width:3 * width])
    v = _dot(h, win_ref[:, 3 * width:4 * width])
    glu = u * _sigmoid(v)
    glu_ref[...] = glu.reshape(tn, nb, width)

    def conv_row(j):
        return sc_ref[j] if j < CONV_BUF else glu[(j - CONV_BUF) * nb:(j - CONV_BUF + 1) * nb, :]

    c_rows = []
    for ti in range(tn):
        acc = jnp.zeros((nb, width), F32)
        for k in range(CONV_K):
            acc = acc + conv_row(ti + k) * cw_ref[k:k + 1, :]
        c_rows.append(acc)
    c = _layernorm_silu(jnp.concatenate(c_rows, axis=0) + cb_ref[...], lnw_ref[...], lnb_ref[...])
    gb = _dot(h, win_ref[:, 4 * width:5 * width])
    mix_b = (c * _silu(gb)).astype(BF16)

    y = x + _dot(mix_a, wout_ref[0:width, :]) + _dot(mix_b, wout_ref[width:2 * width, :])
    y_ref[...] = y.reshape(tn, nb, y_ref.shape[-1])


def _ab_sample(x_t, sp_t, sc_t, nw, win, pw, ps, cw, cb, lnw, lnb, wout):
    tn, nb, d = x_t.shape
    width = ps.shape[-1]
    kern = functools.partial(_ab_sample_kernel, tn=tn, nb=nb, width=width)
    args = (x_t, sp_t, sc_t, nw, win, pw, ps, cw, cb, lnw, lnb, wout)
    return pl.pallas_call(
        kern,
        grid=(1,),
        in_specs=[_const_spec(a.shape) for a in args],
        out_specs=[_const_spec((tn, nb, d)), _const_spec((tn, nb, width)), _const_spec((tn, nb, width))],
        out_shape=[jax.ShapeDtypeStruct((tn, nb, d), F32),
                   jax.ShapeDtypeStruct((tn, nb, width), F32),
                   jax.ShapeDtypeStruct((tn, nb, width), F32)],
        compiler_params=_params(("arbitrary",)),
        name="ab_sample",
    )(*args)


def _subhead_norm(x, w2):
    lane = lax.broadcasted_iota(jnp.int32, x.shape, 1)
    lo = lane < HEAD_DIM
    sq = x * x
    s_lo = jnp.sum(jnp.where(lo, sq, 0.0), axis=-1, keepdims=True)
    s_hi = jnp.sum(jnp.where(lo, 0.0, sq), axis=-1, keepdims=True)
    ms = jnp.where(lo, s_lo, s_hi) * (1.0 / HEAD_DIM)
    return x * lax.rsqrt(ms + EPS) * w2


def _qkv_kernel(x_ref, nw_ref, win_ref, qw_ref, kw_ref, *out_refs, cw, head_major):
    x = x_ref[...]
    h = _rms_rows(x, nw_ref[...]).astype(BF16)
    q = _dot(h, win_ref[:, 0:cw])
    k = _dot(h, win_ref[:, cw:2 * cw])
    v = _dot(h, win_ref[:, 2 * cw:3 * cw])
    g = _dot(h, win_ref[:, 3 * cw:4 * cw])
    if head_major:
        k_ref, v_ref, g_ref, q1_ref, q2_ref, kb_ref, vb_ref = out_refs
    else:
        k_ref, v_ref, g_ref, q_ref = out_refs
    v_ref[...] = v
    g_ref[...] = g.astype(g_ref.dtype)
    for hd in range(N_HEADS):
        sl = slice(hd * V_DIM, (hd + 1) * V_DIM)
        qn = _subhead_norm(q[:, sl], qw_ref[...]) * (ATTN_SCALE * LOG2E)
        kn = _subhead_norm(k[:, sl], kw_ref[...])
        k_ref[:, sl] = kn
        if head_major:
            qt = qn.T
            lo = lax.broadcasted_iota(jnp.int32, qt.shape, 0) < HEAD_DIM
            q1_ref[hd, 0] = jnp.where(lo, qt, 0.0).astype(BF16)
            q2_ref[hd, 0] = jnp.where(lo, 0.0, qt).astype(BF16)
            kb_ref[hd] = kn.astype(BF16)
            vb_ref[hd, 0] = v[:, sl].T.astype(BF16)
        else:
            q_ref[:, sl] = qn


def _qkv(x2d, nw, win, qw2, kw2, *, tt, head_major, attn_block=None):
    n, d = x2d.shape
    cw = win.shape[1] // 4
    tile = pl.BlockSpec((tt, d), lambda i: (i, 0))
    wide = pl.BlockSpec((tt, cw), lambda i: (i, 0))
    out_specs = [wide, wide, wide]
    out_shape = [jax.ShapeDtypeStruct((n, cw), F32)] * 2 + [jax.ShapeDtypeStruct((n, cw), BF16)]
    if head_major:
        per = attn_block // tt
        hm = pl.BlockSpec((N_HEADS, tt, V_DIM), lambda i: (0, i, 0))
        hmt = pl.BlockSpec((N_HEADS, 1, V_DIM, tt), lambda i: (0, i // per, 0, i % per))
        out_specs += [hmt, hmt, hm, hmt]
        tshape = jax.ShapeDtypeStruct((N_HEADS, n // attn_block, V_DIM, attn_block), BF16)
        out_shape += [tshape, tshape, jax.ShapeDtypeStruct((N_HEADS, n, V_DIM), BF16), tshape]
    else:
        out_specs += [wide]
        out_shape += [jax.ShapeDtypeStruct((n, cw), F32)]
    return pl.pallas_call(
        functools.partial(_qkv_kernel, cw=cw, head_major=head_major),
        grid=(n // tt,),
        in_specs=[tile, _const_spec(nw.shape), _const_spec(win.shape), _const_spec(qw2.shape),
                  _const_spec(kw2.shape)],
        out_specs=out_specs,
        out_shape=out_shape,
        compiler_params=_params(("arbitrary",)),
        name="qkv_prompt" if head_major else "qkv_sample",
    )(x2d, nw, win, qw2, kw2)


def _lambda(lq1_ref, lk1_ref, lq2_ref, lk2_ref, lambda_init):
    a = jnp.sum(lq1_ref[...] * lk1_ref[...], axis=-1, keepdims=True)
    b = jnp.sum(lq2_ref[...] * lk2_ref[...], axis=-1, keepdims=True)
    return jnp.exp(a) - jnp.exp(b) + lambda_init


def _fold_sublanes(x, op):
    shift = SUBLANES // 2
    while shift:
        x = op(x, pltpu.roll(x, shift, 0))
        shift //= 2
    return x


def _col_reduce(x3, op, reduce_leading):
    return _fold_sublanes(reduce_leading(x3, axis=0), op)


def _prompt_attn_step(qi, q1_ref, q2_ref, k_ref, v_ref, lam, sw_ref, o_ref, m1, l1, a1, m2, l2, a2, *,
                      tq, lambda_init, co_scores, co_finish):
    q1 = q1_ref[0, 0]
    q2 = q2_ref[0, 0]
    tkp = tq // KV_SPLIT
    nd = V_DIM // SUBLANES

    for m, l, a in ((m1, l1, a1), (m2, l2, a2)):
        m[...] = jnp.full(m.shape, NEG_INF, F32)
        l[...] = jnp.zeros(l.shape, F32)
        a[...] = jnp.zeros(a.shape, F32)

    def scores(j, diagonal=False):
        chains = []
        ones = jnp.ones((2 * SUBLANES, tkp), BF16)
        for part in range(KV_SPLIT):
            lo = part * tkp
            q_lo = lo if diagonal else 0
            k = k_ref[0, pl.ds(pl.multiple_of(j * tq, tq) + lo, tkp), :]
            vt = jnp.concatenate([v_ref[0, j, :, lo:lo + tkp], ones], axis=0)
            for q, state in ((q1, (m1, l1, a1)), (q2, (m2, l2, a2))):
                chains.append((_dot(k, q[:, q_lo:]), vt, lo, q_lo, state))
        return chains

    def softmax_pv(chains, masked):
        for s, vt, lo, q_lo, (m, l, a) in chains:
            nq = tq - q_lo
            if masked:
                key = lo + lax.broadcasted_iota(jnp.int32, s.shape, 0)
                qry = q_lo + lax.broadcasted_iota(jnp.int32, s.shape, 1)
                s = jnp.where(key <= qry, s, NEG_INF)
            s3 = s.reshape(tkp // SUBLANES, SUBLANES, nq)
            m_old = m[:, q_lo:]
            m_new = jnp.maximum(m_old, _col_reduce(s3, jnp.maximum, jnp.max))
            p = jnp.exp2(s3 - m_new[None]).reshape(tkp, nq).astype(BF16)
            alpha = jnp.exp2(m_old - m_new)
            pv = _dot(vt, p)
            l[:, q_lo:] = alpha * l[:, q_lo:] + pv[V_DIM:V_DIM + SUBLANES, :]
            a[:, q_lo:] = ((alpha[None] * a[:, q_lo:].reshape(nd, SUBLANES, nq)).reshape(V_DIM, nq)
                           + pv[0:V_DIM, :])
            m[:, q_lo:] = m_new

    def body(jj, carry):
        base = jj * LOOP_BLOCKS
        chains = []
        for u in range(LOOP_BLOCKS):
            chains += scores(base + u)
        softmax_pv(chains, False)
        return carry

    n_prev = jnp.minimum(qi, FINAL_BLOCKS)
    n_loop = qi - n_prev
    n_multi = n_loop // LOOP_BLOCKS
    lax.fori_loop(0, n_multi, body, 0)
    for u in range(LOOP_BLOCKS - 1):
        @pl.when(n_multi * LOOP_BLOCKS + u < n_loop)
        def _():
            softmax_pv(scores(n_multi * LOOP_BLOCKS + u), False)

    def final(prev):
        plain = []
        for u in range(prev):
            plain += scores(qi - prev + u)
        masked = scores(qi, diagonal=True)
        co = co_scores()
        softmax_pv(plain, False)
        softmax_pv(masked, True)
        co_finish(co)

    for prev in range(FINAL_BLOCKS + 1):
        @pl.when(n_prev == prev)
        def _(prev=prev):
            final(prev)

    o3 = (a1[...].reshape(nd, SUBLANES, tq) * (1.0 / l1[...])[None]
          - lam * (a2[...].reshape(nd, SUBLANES, tq) * (1.0 / l2[...])[None]))
    ms = _col_reduce(o3 * o3, jnp.add, jnp.sum) * (1.0 / V_DIM)
    on = (o3 * lax.rsqrt(ms + EPS)[None]).reshape(V_DIM, tq)
    o_ref[...] = (on.T * sw_ref[...] * (1.0 - lambda_init)).astype(o_ref.dtype)


def _sample_attn_step(step, last, q_ref, kn_ref, vn_ref, k_refs, v_refs, lam, sw_ref, o_ref,
                      wq, newk, newv, m_ref, l_ref, acc, *, tn, lambda_init):
    half = tn * N_HEADS
    page_rows = PAGE * N_HEADS

    def same_head(shape):
        row = lax.broadcasted_iota(jnp.int32, shape, 0)
        col = lax.broadcasted_iota(jnp.int32, shape, 1)
        return (col & (N_HEADS - 1)) == (row & (N_HEADS - 1))

    def update(s_list, v_list):
        m_old = m_ref[...]
        m_new = m_old
        for s in s_list:
            m_new = jnp.maximum(m_new, jnp.max(s, axis=-1, keepdims=True))
        alpha = jnp.exp2(m_old - m_new)
        l_new = alpha * l_ref[...]
        a_new = alpha * acc[...]
        for s, v in zip(s_list, v_list):
            p = jnp.exp2(s - m_new)
            l_new = l_new + jnp.sum(p, axis=-1, keepdims=True)
            a_new = a_new + _dot(p.astype(BF16), v)
        m_ref[...] = m_new
        l_ref[...] = l_new
        acc[...] = a_new

    def start():
        @pl.when(step == 0)
        def _():
            qb = q_ref[0]
            lo = lax.broadcasted_iota(jnp.int32, qb.shape, 1) < HEAD_DIM
            wq[...] = jnp.concatenate([jnp.where(lo, qb, 0.0), jnp.where(lo, 0.0, qb)], axis=0).astype(BF16)
            newk[...] = jnp.zeros(newk.shape, F32)
            newv[...] = jnp.zeros(newv.shape, F32)
            newk[0:half, :] = kn_ref[0]
            newv[0:half, :] = vn_ref[0]
            m_ref[...] = jnp.full(m_ref.shape, NEG_INF, F32)
            l_ref[...] = jnp.zeros(l_ref.shape, F32)
            acc[...] = jnp.zeros(acc.shape, F32)
            s = _dot_nt(wq[...], newk[...].astype(BF16))
            row = lax.broadcasted_iota(jnp.int32, s.shape, 0)
            col = lax.broadcasted_iota(jnp.int32, s.shape, 1)
            causal = (col >> 3) <= ((row >> 3) & (tn - 1))
            s = jnp.where(same_head(s.shape), jnp.where(causal, s, NEG_INF), NEG_INF)
            update([s], [newv[...].astype(BF16)])

    def scores():
        w = wq[...]
        return [_dot_nt(w, kr[...].reshape(page_rows, V_DIM).astype(BF16)) for kr in k_refs]

    def finish(raw):
        mask = same_head((2 * half, page_rows))
        update([jnp.where(mask, s, NEG_INF) for s in raw],
               [vr[...].reshape(page_rows, V_DIM).astype(BF16) for vr in v_refs])

        @pl.when(step == last)
        def _():
            o = acc[...] / l_ref[...]
            od = o[0:half, :] - lam * o[half:2 * half, :]
            o_ref[0] = (_rms_rows(od, sw_ref[...]) * (1.0 - lambda_init)).astype(o_ref.dtype)

    return start, scores, finish


def _attn_kernel(pt_ref, q1_ref, q2_ref, k_ref, v_ref, lq1_ref, lk1_ref, lq2_ref, lk2_ref, sw_ref,
                 dq_ref, dkn_ref, dvn_ref, *refs, pages, chunks, tq, tn, lambda_init):
    k_refs = refs[:pages]
    v_refs = refs[pages:2 * pages]
    o_ref, do_ref = refs[2 * pages:2 * pages + 2]
    prompt_state = refs[2 * pages + 2:2 * pages + 8]
    sample_state = refs[2 * pages + 8:]
    lam = _lambda(lq1_ref, lk1_ref, lq2_ref, lk2_ref, lambda_init)
    qi = pl.program_id(2)
    lin = (pl.program_id(0) * pl.num_programs(1) + pl.program_id(1)) * pl.num_programs(2) + qi
    start, scores, finish = _sample_attn_step(lin % chunks, chunks - 1, dq_ref, dkn_ref, dvn_ref, k_refs, v_refs,
                                              lam, sw_ref, do_ref, *sample_state, tn=tn, lambda_init=lambda_init)
    start()
    _prompt_attn_step(qi, q1_ref, q2_ref, k_ref, v_ref, lam, sw_ref, o_ref, *prompt_state,
                      tq=tq, lambda_init=lambda_init, co_scores=scores, co_finish=finish)


def _attn(page_table, q1, q2, kb, vb, dq, dk_new, dv_new, cache_k, cache_v, lq1, lk1, lq2, lk2, sw, *,
          nb, t, tq, layer, pages, lambda_init):
    nq = t // tq
    nd, half, _ = dq.shape
    tn = half // N_HEADS
    n_pages = page_table.shape[1]
    chunks = n_pages // pages
    assert nb * N_HEADS * nq == nd * chunks, "one chunk of sample pages per prompt grid step"

    def lin(b, h, i):
        return (b * N_HEADS + h) * nq + i

    qspec = pl.BlockSpec((1, 1, V_DIM, tq), lambda b, h, i, pt: (h, b * nq + i, 0, 0))
    kspec = pl.BlockSpec((1, t, V_DIM), lambda b, h, i, pt: (h, b, 0))
    vspec = pl.BlockSpec((1, nq, V_DIM, tq), lambda b, h, i, pt: (h, b, 0, 0))
    vec = pl.BlockSpec(lq1.shape, lambda b, h, i, pt: (0, 0))
    tok = pl.BlockSpec((1, half, V_DIM), lambda b, h, i, pt: (lin(b, h, i) // chunks, 0, 0))

    def page_spec(p):
        def index(b, h, i, pt):
            step = lin(b, h, i)
            return (layer, pt[(step // chunks) * n_pages + (step % chunks) * pages + p], 0, 0, 0)
        return pl.BlockSpec((None, None, PAGE, N_HEADS, V_DIM), index)

    rows = 2 * half
    grid_spec = pltpu.PrefetchScalarGridSpec(
        num_scalar_prefetch=1,
        grid=(nb, N_HEADS, nq),
        in_specs=[qspec, qspec, kspec, vspec, vec, vec, vec, vec,
                  pl.BlockSpec(sw.shape, lambda b, h, i, pt: (0, 0)), tok, tok, tok]
                 + [page_spec(p) for p in range(pages)] * 2,
        out_specs=[pl.BlockSpec((tq, V_DIM), lambda b, h, i, pt: (b * nq + i, h)), tok],
        scratch_shapes=[pltpu.VMEM((SUBLANES, tq), F32), pltpu.VMEM((SUBLANES, tq), F32),
                        pltpu.VMEM((V_DIM, tq), F32)] * 2
                       + [pltpu.VMEM((rows, V_DIM), BF16),
                          pltpu.VMEM((LANES, V_DIM), F32), pltpu.VMEM((LANES, V_DIM), F32),
                          pltpu.VMEM((rows, 1), F32), pltpu.VMEM((rows, 1), F32),
                          pltpu.VMEM((rows, V_DIM), F32)],
    )
    return pl.pallas_call(
        functools.partial(_attn_kernel, pages=pages, chunks=chunks, tq=tq, tn=tn, lambda_init=lambda_init),
        grid_spec=grid_spec,
        out_shape=[jax.ShapeDtypeStruct((nb * t, N_HEADS * V_DIM), BF16),
                   jax.ShapeDtypeStruct((nd, half, V_DIM), BF16)],
        compiler_params=_params(("arbitrary", "arbitrary", "arbitrary")),
        name="attn",
    )(page_table.reshape(-1), q1, q2, kb, vb, lq1, lk1, lq2, lk2, sw, dq, dk_new, dv_new,
      *([cache_k] * pages), *([cache_v] * pages))


def _out_kernel(x_ref, o_ref, g_ref, w_ref, y_ref):
    gated = (o_ref[...].astype(F32) * _silu(g_ref[...].astype(F32))).astype(BF16)
    y_ref[...] = x_ref[...] + _dot(gated, w_ref[...])


def _out_proj(x2d, o, g, w, *, tt):
    n, d = x2d.shape
    tile = pl.BlockSpec((tt, d), lambda i: (i, 0))
    wide = pl.BlockSpec((tt, w.shape[0]), lambda i: (i, 0))
    return pl.pallas_call(
        _out_kernel,
        grid=(n // tt,),
        in_specs=[tile, wide, wide, _const_spec(w.shape)],
        out_specs=tile,
        out_shape=jax.ShapeDtypeStruct((n, d), F32),
        compiler_params=_params(("arbitrary",)),
        name="out_proj",
    )(x2d, o, g, w)


def kernel(x_prompt, x_sample, state_pool, state_conv, cache_k, cache_v, page_table, norm_w_ab, w_in_ab, pool_w, pool_scale, conv_w, conv_b, conv_ln_w, conv_ln_b, w_out_ab, norm_w_c, w_in_c, q_norm_w, k_norm_w, lambda_q1, lambda_k1, lambda_q2, lambda_k2, subln_w, w_out_c):
    nb, t, d = x_prompt.shape
    nd, tn, _ = x_sample.shape
    depth = norm_w_ab.shape[0] + norm_w_c.shape[0]
    row = lambda a: a.reshape(1, -1)

    yp, ys = x_prompt, x_sample
    pool_p, conv_p, k_p, v_p = [], [], [], []
    pool_s, conv_s, k_s, v_s = [], [], [], []
    for l in range(depth):
        j = l // 2
        if l % 2 == 0:
            weights = (row(norm_w_ab[j]), w_in_ab[j].astype(BF16), pool_w[j].astype(BF16), row(pool_scale[j]),
                       conv_w[j], row(conv_b[j]), row(conv_ln_w[j]), row(conv_ln_b[j]), w_out_ab[j].astype(BF16))
            yp, ptail, ctail = _ab_prompt(yp, *weights, tt=256)
            pool_p.append(ptail)
            conv_p.append(ctail)
            ys_t, xa_t, glu_t = _ab_sample(ys.transpose(1, 0, 2), state_pool[j].transpose(1, 0, 2),
                                           state_conv[j].transpose(1, 0, 2), *weights)
            ys = ys_t.transpose(1, 0, 2)
            pool_s.append(jnp.concatenate([state_pool[j], xa_t.transpose(1, 0, 2)], axis=1)[:, -POOL_BUF:])
            conv_s.append(jnp.concatenate([state_conv[j], glu_t.transpose(1, 0, 2)], axis=1)[:, -CONV_BUF:])
        else:
            lambda_init = 0.8 - 0.6 * math.exp(-0.3 * l)
            nw = row(norm_w_c[j])
            win = w_in_c[j].astype(BF16)
            wout = w_out_c[j].astype(BF16)
            qw2 = row(jnp.tile(q_norm_w[j], 2))
            kw2 = row(jnp.tile(k_norm_w[j], 2))
            lams = (row(lambda_q1[j]), row(lambda_k1[j]), row(lambda_q2[j]), row(lambda_k2[j]))
            sw = row(subln_w[j])
            cw = win.shape[1] // 4
            yp2 = yp.reshape(nb * t, d)
            ys2 = ys.reshape(nd * tn, d)
            tq = 512
            kp, vp, gp, q1, q2, kb, vb = _qkv(yp2, nw, win, qw2, kw2, tt=256, head_major=True, attn_block=tq)
            ks, vs, gs, qs = _qkv(ys2, nw, win, qw2, kw2, tt=nd * tn, head_major=False)
            rows_th = lambda a: a.reshape(nd, tn * N_HEADS, V_DIM)
            op, os_ = _attn(page_table, q1, q2, kb, vb, rows_th(qs), rows_th(ks), rows_th(vs), cache_k, cache_v,
                            *lams, sw, nb=nb, t=t, tq=tq, layer=j, pages=8, lambda_init=lambda_init)
            yp = _out_proj(yp2, op, gp, wout, tt=512).reshape(nb, t, d)
            ys = _out_proj(ys2, os_.reshape(nd * tn, cw), gs, wout, tt=nd * tn).reshape(nd, tn, d)
            k_p.append(kp.reshape(nb, t, N_HEADS, V_DIM))
            v_p.append(vp.reshape(nb, t, N_HEADS, V_DIM))
            k_s.append(ks.reshape(nd, tn, N_HEADS, V_DIM))
            v_s.append(vs.reshape(nd, tn, N_HEADS, V_DIM))
    return (yp, ys, jnp.stack(pool_p), jnp.stack(conv_p), jnp.stack(k_p), jnp.stack(v_p),
            jnp.stack(pool_s), jnp.stack(conv_s), jnp.stack(k_s), jnp.stack(v_s))
```
